```python
import jax
import jax.numpy as jnp
from jax import lax
import numpy as np


D_MODEL = 2048
BATCH = 4
SEQ = 8192
DEPTH = 4

CTX_LEN = 256
GRID_W = 64
MIX_WIDTH = D_MODEL
RET_WIDTH = MIX_WIDTH // 2
CONV_WIDTH = MIX_WIDTH - RET_WIDTH
RET_HEAD_DIM = 128
RET_HEADS = RET_WIDTH // RET_HEAD_DIM
RET_CHUNK = 128
CONV_K = 31
FFN_K = 3
D_FF = ((8 * D_MODEL // 3 + 255) // 256) * 256
IN_WIDTH = 4 * RET_WIDTH + 2 * CONV_WIDTH
N_MOD = 6
ROPE_THETA = 10000.0
EPS = 1e-6

kernel_name = "hybrid_retention_conformer_dit"


def rmsnorm(x, g):
    xf = x.astype(jnp.float32)
    xf = xf * lax.rsqrt(jnp.mean(xf * xf, axis=-1, keepdims=True) + EPS)
    return (xf * g.astype(jnp.float32)).astype(x.dtype)


def layernorm(x, g, b):
    xf = x.astype(jnp.float32)
    mu = jnp.mean(xf, axis=-1, keepdims=True)
    var = jnp.mean(jnp.square(xf - mu), axis=-1, keepdims=True)
    y = (xf - mu) * lax.rsqrt(var + EPS) * g.astype(jnp.float32) + b.astype(jnp.float32)
    return y.astype(x.dtype)


def modulate(h, shift, scale):
    return h * (1 + scale) + shift


def rope_tables(pos, dim):
    inv_freq = 1.0 / (ROPE_THETA ** (jnp.arange(0, dim // 2, dtype=jnp.float32) / (dim // 2)))
    ang = pos.astype(jnp.float32)[:, None] * inv_freq[None, :]
    return jnp.cos(ang)[:, None, :], jnp.sin(ang)[:, None, :]


def apply_rope(x, cos, sin):
    half = x.shape[-1] // 2
    x1, x2 = x[..., :half], x[..., half:]
    cos = cos.astype(x.dtype)
    sin = sin.astype(x.dtype)
    return jnp.concatenate([x1 * cos - x2 * sin, x1 * sin + x2 * cos], axis=-1)


def rope_2d(x, row_cs, col_cs):
    half = x.shape[-1] // 2
    return jnp.concatenate([apply_rope(x[..., :half], *row_cs), apply_rope(x[..., half:], *col_cs)], axis=-1)


def dwconv1d(x, w, b):
    y = lax.conv_general_dilated(x, w[:, None, :].astype(x.dtype), window_strides=(1,), padding='SAME',
                                 dimension_numbers=('NWC', 'WIO', 'NWC'), feature_group_count=x.shape[-1])
    return y + b.astype(x.dtype)


def dwconv2d(x, w, b):
    y = lax.conv_general_dilated(x, w[:, :, None, :].astype(x.dtype), window_strides=(1, 1), padding='SAME',
                                 dimension_numbers=('NHWC', 'HWIO', 'NHWC'), feature_group_count=x.shape[-1])
    return y + b.astype(x.dtype)


def retention_chunkwise(q, k, v, log_gamma, s0):
    bsz, nh, length, dh = q.shape
    n_chunks = length // RET_CHUNK
    q = q.astype(jnp.float32)
    k = k.astype(jnp.float32) * (dh ** -0.5)
    v = v.astype(jnp.float32)
    to_chunks = lambda t: jnp.moveaxis(t.reshape(bsz, nh, n_chunks, RET_CHUNK, dh), 2, 0)
    idx = jnp.arange(RET_CHUNK, dtype=jnp.float32)
    diff = idx[:, None] - idx[None, :]
    lg = log_gamma[:, None, None]
    inner_decay = jnp.where(diff >= 0, jnp.exp(lg * jnp.maximum(diff, 0.0)), 0.0)
    q_decay = jnp.exp(log_gamma[:, None] * (idx + 1.0))[:, :, None]
    k_decay = jnp.exp(log_gamma[:, None] * (RET_CHUNK - 1.0 - idx))[:, :, None]
    chunk_decay = jnp.exp(log_gamma * RET_CHUNK)[:, None, None]

    def step(s, qkv):
        qc, kc, vc = qkv
        scores = jnp.einsum('bhnd,bhmd->bhnm', qc, kc) * inner_decay
        o = jnp.einsum('bhnm,bhmd->bhnd', scores, vc) + jnp.einsum('bhnd,bhde->bhne', qc, s) * q_decay
        s = s * chunk_decay + jnp.einsum('bhmd,bhme->bhde', kc * k_decay, vc)
        return s, o

    s_final, o = lax.scan(step, s0.astype(jnp.float32), (to_chunks(q), to_chunks(k), to_chunks(v)))
    o = jnp.moveaxis(o, 0, 2).reshape(bsz, nh, length, dh)
    return o, s_final


def bidir_retention(q, k, v, lg_f, lg_b, s0_f, s0_b):
    o_f, s_f = retention_chunkwise(q, k, v, lg_f, s0_f)
    rev = lambda t: jnp.flip(t, axis=2)
    o_b, s_b = retention_chunkwise(rev(q), rev(k), rev(v), lg_b, s0_b)
    return o_f + rev(o_b), s_f, s_b


def retention_output(o, g):
    o = o * lax.rsqrt(jnp.mean(o * o, axis=-1, keepdims=True) + EPS)
    bsz, nh, length, dh = o.shape
    o = jnp.transpose(o, (0, 2, 1, 3)).reshape(bsz, length, nh * dh)
    return (jax.nn.silu(g.astype(jnp.float32)) * o).astype(g.dtype)


def to_heads(t):
    bsz, length, _ = t.shape
    return t.reshape(bsz, length, RET_HEADS, RET_HEAD_DIM)


def heads_first(t):
    return jnp.transpose(t, (0, 2, 1, 3))


def conformer_conv(a, b, w, bias, ln_g, ln_b):
    u = a * jax.nn.sigmoid(b)
    u = dwconv1d(u, w, bias)
    return jax.nn.silu(layernorm(u, ln_g, ln_b))


def conv_ffn(h, w_up, w_down, conv):
    gate, val = jnp.split(h @ w_up, 2, axis=-1)
    return (jax.nn.silu(conv(gate)) * val) @ w_down


def split_mixer_proj(p):
    return jnp.split(p, [RET_WIDTH, 2 * RET_WIDTH, 3 * RET_WIDTH, 4 * RET_WIDTH, 4 * RET_WIDTH + CONV_WIDTH], axis=-1)


def setup_inputs(seed: int = 0) -> dict:
    key = jax.random.key(seed)
    ks = jax.random.split(key, 24)
    f32 = jnp.float32
    nrm = lambda k, shape, s: s * jax.random.normal(k, shape, f32)
    decay_base = jnp.log(jnp.exp2(5.0 + jnp.arange(RET_HEADS, dtype=f32)) - 1.0)
    return {
        'x': nrm(ks[0], (BATCH, SEQ, D_MODEL), 1.0),
        'c': nrm(ks[1], (BATCH, D_MODEL), 1.0),
        'ctx': nrm(ks[2], (BATCH, CTX_LEN, D_MODEL), 1.0),
        'c_ctx': nrm(ks[3], (D_MODEL,), 1.0),
        'w_mod': nrm(ks[4], (DEPTH, D_MODEL, N_MOD * D_MODEL), 0.5 * D_MODEL ** -0.5),
        'b_mod': nrm(ks[5], (DEPTH, N_MOD * D_MODEL), 0.02),
        'norm1_g': 1.0 + nrm(ks[6], (DEPTH, D_MODEL), 0.02),
        'norm2_g': 1.0 + nrm(ks[7], (DEPTH, D_MODEL), 0.02),
        'w_in': nrm(ks[8], (DEPTH, D_MODEL, IN_WIDTH), D_MODEL ** -0.5),
        'ret_decay_f': decay_base + nrm(ks[9], (DEPTH, RET_HEADS), 0.1),
        'ret_decay_b': decay_base + nrm(ks[10], (DEPTH, RET_HEADS), 0.1),
        'conv_dw_w': nrm(ks[11], (DEPTH, CONV_K, CONV_WIDTH), CONV_K ** -0.5),
        'conv_dw_b': nrm(ks[12], (DEPTH, CONV_WIDTH), 0.02),
        'conv_ln_g': 1.0 + nrm(ks[13], (DEPTH, CONV_WIDTH), 0.02),
        'conv_ln_b': nrm(ks[14], (DEPTH, CONV_WIDTH), 0.02),
        'w_out': nrm(ks[15], (DEPTH, MIX_WIDTH, D_MODEL), MIX_WIDTH ** -0.5),
        'ffn_w_up': nrm(ks[16], (DEPTH, D_MODEL, 2 * D_FF), D_MODEL ** -0.5),
        'ffn_dw_w': nrm(ks[17], (DEPTH, FFN_K, FFN_K, D_FF), 1.0 / FFN_K),
        'ffn_dw_b': nrm(ks[18], (DEPTH, D_FF), 0.02),
        'ffn_w_down': nrm(ks[19], (DEPTH, D_FF, D_MODEL), D_FF ** -0.5),
        'final_norm_g': 1.0 + nrm(ks[20], (D_MODEL,), 0.02),
    }


def reference(x, c, ctx, c_ctx, w_mod, b_mod, norm1_g, norm2_g, w_in, ret_decay_f, ret_decay_b,
              conv_dw_w, conv_dw_b, conv_ln_g, conv_ln_b, w_out, ffn_w_up, ffn_dw_w, ffn_dw_b,
              ffn_w_down, final_norm_g):
    bsz, length, _ = x.shape
    rows = length // GRID_W
    t = jnp.arange(length)
    row_cs = rope_tables(t // GRID_W, RET_HEAD_DIM // 2)
    col_cs = rope_tables(t % GRID_W, RET_HEAD_DIM // 2)
    silu_c = jax.nn.silu(c)
    silu_cc = jax.nn.silu(c_ctx)[None, :]
    zero_state = jnp.zeros((bsz, RET_HEADS, RET_HEAD_DIM, RET_HEAD_DIM), jnp.float32)
    xc = ctx
    for l in range(DEPTH):
        last = l == DEPTH - 1
        mod = (silu_c @ w_mod[l] + b_mod[l])[:, None, :]
        mod_c = (silu_cc @ w_mod[l] + b_mod[l])[:, None, :]
        sh1, sc1, g1, sh2, sc2, g2 = jnp.split(mod, N_MOD, axis=-1)
        csh1, csc1, cg1, csh2, csc2, cg2 = jnp.split(mod_c, N_MOD, axis=-1)

        hx = modulate(rmsnorm(x, norm1_g[l]), sh1, sc1)
        hc = modulate(rmsnorm(xc, norm1_g[l]), csh1, csc1)
        q, k, v, g, a, bg = split_mixer_proj(hx @ w_in[l])
        cq, ck, cv, cgt, ca, cb = split_mixer_proj(hc @ w_in[l])
        lg_f = jax.nn.log_sigmoid(ret_decay_f[l].astype(jnp.float32))
        lg_b = jax.nn.log_sigmoid(ret_decay_b[l].astype(jnp.float32))

        co, s_f, s_b = bidir_retention(heads_first(to_heads(cq)), heads_first(to_heads(ck)),
                                       heads_first(to_heads(cv)), lg_f, lg_b, zero_state, zero_state)
        q = heads_first(rope_2d(to_heads(q), row_cs, col_cs))
        k = heads_first(rope_2d(to_heads(k), row_cs, col_cs))
        o, _, _ = bidir_retention(q, k, heads_first(to_heads(v)), lg_f, lg_b, s_f, s_b)

        mix = jnp.concatenate([
            retention_output(o, g),
            conformer_conv(a, bg, conv_dw_w[l], conv_dw_b[l], conv_ln_g[l], conv_ln_b[l]),
        ], axis=-1)
        x = x + g1 * (mix @ w_out[l])

        lat_conv = lambda u: dwconv2d(u.reshape(bsz, rows, GRID_W, D_FF), ffn_dw_w[l], ffn_dw_b[l]).reshape(bsz, length, D_FF)
        x = x + g2 * conv_ffn(modulate(rmsnorm(x, norm2_g[l]), sh2, sc2), ffn_w_up[l], ffn_w_down[l], lat_conv)

        if not last:
            cmix = jnp.concatenate([
                retention_output(co, cgt),
                conformer_conv(ca, cb, conv_dw_w[l], conv_dw_b[l], conv_ln_g[l], conv_ln_b[l]),
            ], axis=-1)
            xc = xc + cg1 * (cmix @ w_out[l])
            ctx_conv = lambda u: dwconv1d(u, ffn_dw_w[l][FFN_K // 2], ffn_dw_b[l])
            xc = xc + cg2 * conv_ffn(modulate(rmsnorm(xc, norm2_g[l]), csh2, csc2), ffn_w_up[l], ffn_w_down[l], ctx_conv)

    return rmsnorm(x, final_norm_g)
```

```python
import functools

import jax
import jax.numpy as jnp
from jax import lax
from jax.experimental import pallas as pl
from jax.experimental.pallas import tpu as pltpu

F32 = jnp.float32
BF16 = jnp.bfloat16

GRID_W = 64
HEAD_DIM = 128
CHUNK = 128
N_MOD = 6
ROPE_THETA = 10000.0
EPS = 1e-6
MOD_ROWS = 8
CONV_HALO = 16
ROW_BLOCK = 16
VMEM_LIMIT = 52 * 1024 * 1024

ACT_DTYPE = BF16


def _pick(n, candidates):
    for c in candidates:
        if n % c == 0:
            return c
    raise ValueError(f"no tile for {n} in {candidates}")


def _params(sem):
    return pltpu.CompilerParams(dimension_semantics=sem, vmem_limit_bytes=VMEM_LIMIT)


def _sigmoid(x):
    return jax.nn.sigmoid(x)


def _norm_mod(x, g, sh, sc):
    xn = x * lax.rsqrt(jnp.mean(x * x, axis=-1, keepdims=True) + EPS)
    return (xn * g) * (1.0 + sc) + sh


def _mod_kernel(c_ref, w_ref, b_ref, o_ref):
    c = c_ref[...]
    s = c * _sigmoid(c)
    o_ref[...] = jnp.dot(s.astype(BF16), w_ref[...].astype(BF16), preferred_element_type=F32) + b_ref[...]


def _modulation(cs, w_mod, b_mod):
    depth, d, n = w_mod.shape
    tn = _pick(n, (1024, 512, 256, 128))
    return pl.pallas_call(
        _mod_kernel,
        grid=(depth, n // tn),
        in_specs=[
            pl.BlockSpec((MOD_ROWS, d), lambda l, j: (0, 0)),
            pl.BlockSpec((None, d, tn), lambda l, j: (l, 0, j)),
            pl.BlockSpec((None, 1, tn), lambda l, j: (l, 0, j)),
        ],
        out_specs=pl.BlockSpec((None, MOD_ROWS, tn), lambda l, j: (l, 0, j)),
        out_shape=jax.ShapeDtypeStruct((depth, MOD_ROWS, n), F32),
        compiler_params=_params(("arbitrary", "arbitrary")),
        name="modulation",
    )(cs, w_mod, b_mod.reshape(depth, 1, n))


def _mod_spec(layer, which, tm, seq, batch, d):
    def imap(i, *_):
        grp = jnp.minimum((i * tm) // seq, batch)
        return ((layer * MOD_ROWS + grp) * N_MOD + which, 0, 0)
    return pl.BlockSpec((None, 1, d), imap)


def _norm_mod_kernel(x_ref, g_ref, sh_ref, sc_ref, h_ref):
    g, sh, sc = g_ref[...], sh_ref[...], sc_ref[...]

    def body(r, carry):
        rows = pl.ds(pl.multiple_of(r * ROW_BLOCK, ROW_BLOCK), ROW_BLOCK)
        h_ref[rows, :] = _norm_mod(x_ref[rows, :], g, sh, sc).astype(h_ref.dtype)
        return carry

    lax.fori_loop(0, x_ref.shape[0] // ROW_BLOCK, body, 0)


def _first_norm(x_all, norm_g, mods, layer, tm, seq, batch):
    t, d = x_all.shape
    return pl.pallas_call(
        _norm_mod_kernel,
        grid=(t // tm,),
        in_specs=[
            pl.BlockSpec((tm, d), lambda i: (i, 0)),
            pl.BlockSpec((None, 1, d), lambda i: (layer, 0, 0)),
            _mod_spec(layer, 0, tm, seq, batch, d),
            _mod_spec(layer, 1, tm, seq, batch, d),
        ],
        out_specs=pl.BlockSpec((tm, d), lambda i: (i, 0)),
        out_shape=jax.ShapeDtypeStruct((t, d), BF16),
        compiler_params=_params(("arbitrary",)),
        name="first_norm",
    )(x_all, norm_g, mods, mods)


def _proj_kernel(a_ref, w_ref, cos_ref, sin_ref, o_ref, *, n_rope_tiles, n_rope_blocks):
    acc = jnp.dot(a_ref[...], w_ref[...], preferred_element_type=F32)
    tm, tn = acc.shape
    if n_rope_blocks == 0:
        o_ref[...] = acc.astype(o_ref.dtype)
        return
    do_rope = jnp.logical_and(pl.program_id(0) < n_rope_tiles, pl.program_id(1) < n_rope_blocks)

    @pl.when(do_rope)
    def _():
        cos, sin = cos_ref[...], sin_ref[...]
        lane = lax.broadcasted_iota(jnp.int32, (tm, HEAD_DIM), 1)
        first = (lane % (HEAD_DIM // 2)) < (HEAD_DIM // 4)
        for hh in range(tn // HEAD_DIM):
            xs = acc[:, hh * HEAD_DIM:(hh + 1) * HEAD_DIM]
            partner = jnp.where(first, pltpu.roll(xs, HEAD_DIM - HEAD_DIM // 4, 1), pltpu.roll(xs, HEAD_DIM // 4, 1))
            o_ref[:, hh * HEAD_DIM:(hh + 1) * HEAD_DIM] = (xs * cos + partner * sin).astype(o_ref.dtype)

    @pl.when(jnp.logical_not(do_rope))
    def _():
        o_ref[...] = acc.astype(o_ref.dtype)


def _projection(a, w, layer, n_tiles, tm, tn, rope=None):
    t, kdim = a.shape
    n = w.shape[2]
    if rope is None:
        cos = sin = jnp.zeros((8, HEAD_DIM), F32)
        rope_spec = pl.BlockSpec((8, HEAD_DIM), lambda i, j: (0, 0))
        n_rope_tiles = n_rope_blocks = 0
    else:
        cos, sin, n_rope_tiles, n_rope_cols = rope
        tiles_per_seq = cos.shape[0] // tm
        rope_spec = pl.BlockSpec((tm, HEAD_DIM), lambda i, j: (i % tiles_per_seq, 0))
        n_rope_blocks = n_rope_cols // tn
    return pl.pallas_call(
        functools.partial(_proj_kernel, n_rope_tiles=n_rope_tiles, n_rope_blocks=n_rope_blocks),
        grid=(n_tiles, n // tn),
        in_specs=[
            pl.BlockSpec((tm, kdim), lambda i, j: (i, 0)),
            pl.BlockSpec((None, kdim, tn), lambda i, j: (layer, 0, j)),
            rope_spec,
            rope_spec,
        ],
        out_specs=pl.BlockSpec((tm, tn), lambda i, j: (i, j)),
        out_shape=jax.ShapeDtypeStruct((t, n), ACT_DTYPE),
        compiler_params=_params(("arbitrary", "arbitrary")),
        name="projection",
    )(a, w, cos, sin)


def _log_sigmoid(x):
    return jnp.minimum(x, 0.0) - jnp.log1p(jnp.exp(-jnp.abs(x)))


def _retention_kernel(dec_ref, q_ref, k_ref, v_ref, g_ref, s0f_ref, s0b_ref, *rest, nseg, nchunk, aliased):
    if aliased:
        rest = rest[1:]
    o_ref, sf_ref, sb_ref, of_scr, state_scr, decay_scr = rest
    s = pl.program_id(2)
    c_len = CHUNK
    lseg = nchunk * c_len
    scale = HEAD_DIM ** -0.5

    @pl.when(s == 0)
    def _init():
        row = lax.broadcasted_iota(jnp.int32, (c_len, HEAD_DIM), 0).astype(F32)
        col = lax.broadcasted_iota(jnp.int32, (c_len, HEAD_DIM), 1).astype(F32)
        for d in range(2):
            lg = _log_sigmoid(dec_ref[d])
            diff = (row - col) if d == 0 else (col - row)
            decay_scr[d, 0] = jnp.where(diff >= 0, jnp.exp(lg * jnp.maximum(diff, 0.0)), 0.0)
            if d == 0:
                decay_scr[d, 1] = jnp.exp(lg * (row + 1.0))
                decay_scr[d, 2] = jnp.exp(lg * (c_len - 1.0 - row))
            else:
                decay_scr[d, 1] = jnp.exp(lg * (c_len - row))
                decay_scr[d, 2] = jnp.exp(lg * row)
            decay_scr[d, 3] = jnp.broadcast_to(jnp.exp(lg * float(c_len)), (c_len, HEAD_DIM))
        state_scr[...] = s0f_ref[...]

    @pl.when(s == nseg)
    def _init_b():
        state_scr[...] = s0b_ref[...]

    def chunk(c, d, state):
        rows = slice(c * c_len, (c + 1) * c_len)
        q = q_ref[rows, :].astype(BF16)
        kf = k_ref[rows, :].astype(F32) * scale
        v = v_ref[rows, :].astype(BF16)
        sc = lax.dot_general(q, kf.astype(BF16), (((1,), (1,)), ((), ())), preferred_element_type=F32)
        sc = sc * decay_scr[d, 0]
        o = jnp.dot(sc.astype(BF16), v, preferred_element_type=F32)
        o = o + jnp.dot(q, state.astype(BF16), preferred_element_type=F32) * decay_scr[d, 1]
        kd = (kf * decay_scr[d, 2]).astype(BF16)
        state = state * decay_scr[d, 3] + lax.dot_general(kd, v, (((0,), (0,)), ((), ())), preferred_element_type=F32)
        return o, state

    @pl.when(s < nseg)
    def _fwd():
        state = state_scr[...]
        for c in range(nchunk):
            o, state = chunk(c, 0, state)
            of_scr[pl.ds(pl.multiple_of(s * lseg + c * c_len, c_len), c_len), :] = o
        state_scr[...] = state

        @pl.when(s == nseg - 1)
        def _():
            sf_ref[...] = state

    @pl.when(s >= nseg)
    def _bwd():
        seg = 2 * nseg - 1 - s
        state = state_scr[...]
        for c in reversed(range(nchunk)):
            o, state = chunk(c, 1, state)
            o = o + of_scr[pl.ds(pl.multiple_of(seg * lseg + c * c_len, c_len), c_len), :]
            o = o * lax.rsqrt(jnp.mean(o * o, axis=-1, keepdims=True) + EPS)
            g = g_ref[c * c_len:(c + 1) * c_len, :].astype(F32)
            o_ref[c * c_len:(c + 1) * c_len, :] = (g * _sigmoid(g) * o).astype(o_ref.dtype)
        state_scr[...] = state

        @pl.when(s == 2 * nseg - 1)
        def _():
            sb_ref[...] = state


def _retention(p, dec, layer, s0f, s0b, mix, *, base_row, nseq, seq, heads, mix_width):
    t = p.shape[0]
    lseg = _pick(seq, (1024, 512, 256, 128))
    nseg = seq // lseg
    nchunk = lseg // CHUNK
    base_blk = base_row // lseg
    assert base_row % lseg == 0

    def seg_of(s):
        return jnp.where(s < nseg, s, 2 * nseg - 1 - s)

    def out_seg_of(s):
        return jnp.where(s < nseg, nseg - 1, 2 * nseg - 1 - s)

    def in_spec(col0):
        return pl.BlockSpec((lseg, HEAD_DIM), lambda b, h, s: (base_blk + b * nseg + seg_of(s), col0 + h))

    state_spec = pl.BlockSpec((None, None, HEAD_DIM, HEAD_DIM), lambda b, h, s: (b, h, 0, 0))
    in_specs = [
        pl.BlockSpec((None, 2, None, 1, HEAD_DIM), lambda b, h, s: (layer, 0, h, 0, 0)),
        in_spec(0), in_spec(heads), in_spec(2 * heads),
        pl.BlockSpec((lseg, HEAD_DIM), lambda b, h, s: (base_blk + b * nseg + out_seg_of(s), 3 * heads + h)),
        state_spec, state_spec,
    ]
    args = [dec, p, p, p, p, s0f, s0b]
    aliases = {}
    if mix is not None:
        in_specs.append(pl.BlockSpec(memory_space=pl.ANY))
        args.append(mix)
        aliases = {len(args) - 1: 0}
    state_shape = jax.ShapeDtypeStruct((nseq, heads, HEAD_DIM, HEAD_DIM), F32)
    return pl.pallas_call(
        functools.partial(_retention_kernel, nseg=nseg, nchunk=nchunk, aliased=mix is not None),
        grid=(nseq, heads, 2 * nseg),
        in_specs=in_specs,
        out_specs=[
            pl.BlockSpec((lseg, HEAD_DIM), lambda b, h, s: (base_blk + b * nseg + out_seg_of(s), h)),
            state_spec, state_spec,
        ],
        out_shape=[jax.ShapeDtypeStruct((t, mix_width), BF16), state_shape, state_shape],
        scratch_shapes=[
            pltpu.VMEM((seq, HEAD_DIM), F32),
            pltpu.VMEM((HEAD_DIM, HEAD_DIM), F32),
            pltpu.VMEM((2, 4, CHUNK, HEAD_DIM), F32),
        ],
        input_output_aliases=aliases,
        compiler_params=_params(("arbitrary", "arbitrary", "arbitrary")),
        name="retention",
    )(*args)


def _cconv_kernel(am_ref, ap_ref, an_ref, bm_ref, bp_ref, bn_ref, w_ref, b_ref, lg_ref, lb_ref, mix_ref,
                  o_ref, u_scr, y_scr, *, taps):
    del mix_ref
    j = pl.program_id(1)
    nj = pl.num_programs(1)
    tl, cw = o_ref.shape
    halo = CONV_HALO

    def glu(a_ref, b_ref_):
        return a_ref[...].astype(F32) * _sigmoid(b_ref_[...].astype(F32))

    u_scr[halo:halo + tl, :] = glu(am_ref, bm_ref)
    u_scr[0:halo, :] = jnp.where(j > 0, glu(ap_ref, bp_ref), 0.0)
    u_scr[halo + tl:halo + tl + halo, :] = jnp.where(j < nj - 1, glu(an_ref, bn_ref), 0.0)

    off0 = halo - taps // 2
    rb = 64 if tl % 64 == 0 else tl

    def lane_body(c, carry):
        lanes = pl.ds(pl.multiple_of(c * 128, 128), 128)
        for r in range(tl // rb):
            acc = jnp.zeros((rb, 128), F32)
            for k in range(taps):
                acc = acc + w_ref[k:k + 1, lanes] * u_scr[r * rb + off0 + k:r * rb + off0 + k + rb, lanes]
            y_scr[r * rb:(r + 1) * rb, lanes] = acc
        return carry

    lax.fori_loop(0, cw // 128, lane_body, 0)

    bias, ln_g, ln_b = b_ref[...], lg_ref[...], lb_ref[...]

    def row_body(r, carry):
        rows = pl.ds(pl.multiple_of(r * ROW_BLOCK, ROW_BLOCK), ROW_BLOCK)
        y = y_scr[rows, :] + bias
        mu = jnp.mean(y, axis=-1, keepdims=True)
        yc = y - mu
        var = jnp.mean(yc * yc, axis=-1, keepdims=True)
        z = yc * lax.rsqrt(var + EPS) * ln_g + ln_b
        o_ref[rows, :] = (z * _sigmoid(z)).astype(o_ref.dtype)
        return carry

    lax.fori_loop(0, tl // ROW_BLOCK, row_body, 0)


def _conformer_conv(p, w, b, ln_g, ln_b, layer, mix, *, base_row, nseq, seq, col_a, col_out):
    t = p.shape[0]
    taps, cw = w.shape[1], w.shape[2]
    tl = _pick(seq, (512, 256, 128))
    nt = seq // tl
    assert base_row % tl == 0 and taps // 2 < CONV_HALO
    base_blk = base_row // tl
    hpt = tl // CONV_HALO
    last_halo = t // CONV_HALO - 1

    def main_spec(col):
        return pl.BlockSpec((tl, cw), lambda s, j: (base_blk + s * nt + j, col))

    def prev_spec(col):
        return pl.BlockSpec((CONV_HALO, cw), lambda s, j: (jnp.maximum((base_blk + s * nt + j) * hpt - 1, 0), col))

    def next_spec(col):
        return pl.BlockSpec((CONV_HALO, cw), lambda s, j: (jnp.minimum((base_blk + s * nt + j + 1) * hpt, last_halo), col))

    vec_spec = pl.BlockSpec((None, 1, cw), lambda s, j: (layer, 0, 0))
    return pl.pallas_call(
        functools.partial(_cconv_kernel, taps=taps),
        grid=(nseq, nt),
        in_specs=[
            main_spec(col_a), prev_spec(col_a), next_spec(col_a),
            main_spec(col_a + 1), prev_spec(col_a + 1), next_spec(col_a + 1),
            pl.BlockSpec((None, taps, cw), lambda s, j: (layer, 0, 0)),
            vec_spec, vec_spec, vec_spec,
            pl.BlockSpec(memory_space=pl.ANY),
        ],
        out_specs=pl.BlockSpec((tl, cw), lambda s, j: (base_blk + s * nt + j, col_out)),
        out_shape=jax.ShapeDtypeStruct(mix.shape, mix.dtype),
        scratch_shapes=[pltpu.VMEM((tl + 2 * CONV_HALO, cw), F32), pltpu.VMEM((tl, cw), F32)],
        input_output_aliases={10: 0},
        compiler_params=_params(("arbitrary", "arbitrary")),
        name="conformer_conv",
    )(p, p, p, p, p, p, w, b, ln_g, ln_b, mix)


def _mm_res_kernel(a_ref, w_ref, x_ref, gate_ref, ng_ref, *rest, final):
    if final:
        o_ref, acc_scr = rest
    else:
        sh_ref, sc_ref, xo_ref, h_ref, acc_scr = rest
    k = pl.program_id(1)
    nk = pl.num_programs(1)
    prod = jnp.dot(a_ref[...], w_ref[...], preferred_element_type=F32)

    @pl.when(k == 0)
    def _():
        acc_scr[...] = prod

    @pl.when(k > 0)
    def _():
        acc_scr[...] += prod

    @pl.when(k == nk - 1)
    def _():
        gate, ng = gate_ref[...], ng_ref[...]
        if not final:
            sh, sc = sh_ref[...], sc_ref[...]

        def body(r, carry):
            rows = pl.ds(pl.multiple_of(r * ROW_BLOCK, ROW_BLOCK), ROW_BLOCK)
            xn = x_ref[rows, :] + gate * acc_scr[rows, :]
            if final:
                o_ref[rows, :] = xn * lax.rsqrt(jnp.mean(xn * xn, axis=-1, keepdims=True) + EPS) * ng
            else:
                xo_ref[rows, :] = xn
                h_ref[rows, :] = _norm_mod(xn, ng, sh, sc).astype(h_ref.dtype)
            return carry

        lax.fori_loop(0, x_ref.shape[0] // ROW_BLOCK, body, 0)


def _mm_residual(a, w, layer, x_all, mods, gate_which, norm_g, norm_layer, next_mods, *, n_tiles, tm, tk, seq, batch,
                 final_rows=None):
    t, d = x_all.shape
    kdim = a.shape[1]
    final = next_mods is None
    in_specs = [
        pl.BlockSpec((tm, tk), lambda i, k: (i, k)),
        pl.BlockSpec((None, tk, d), lambda i, k: (layer, k, 0)),
        pl.BlockSpec((tm, d), lambda i, k: (i, 0)),
        _mod_spec(layer, gate_which, tm, seq, batch, d),
    ]
    args = [a, w, x_all, mods]
    if final:
        in_specs.append(pl.BlockSpec((1, d), lambda i, k: (0, 0)))
        args.append(norm_g.reshape(1, d))
        out_specs = pl.BlockSpec((tm, d), lambda i, k: (i, 0))
        out_shape = jax.ShapeDtypeStruct((final_rows, d), F32)
        aliases = {}
    else:
        nl, shw, scw = next_mods
        in_specs += [
            pl.BlockSpec((None, 1, d), lambda i, k: (norm_layer, 0, 0)),
            _mod_spec(nl, shw, tm, seq, batch, d),
            _mod_spec(nl, scw, tm, seq, batch, d),
        ]
        args += [norm_g, mods, mods]
        out_specs = [pl.BlockSpec((tm, d), lambda i, k: (i, 0)), pl.BlockSpec((tm, d), lambda i, k: (i, 0))]
        out_shape = [jax.ShapeDtypeStruct((t, d), F32), jax.ShapeDtypeStruct((t, d), BF16)]
        aliases = {2: 0}
    return pl.pallas_call(
        functools.partial(_mm_res_kernel, final=final),
        grid=(n_tiles, kdim // tk),
        in_specs=in_specs,
        out_specs=out_specs,
        out_shape=out_shape,
        scratch_shapes=[pltpu.VMEM((tm, d), F32)],
        input_output_aliases=aliases,
        compiler_params=_params(("arbitrary", "arbitrary")),
        name="matmul_residual",
    )(*args)


def _ffn_act_kernel(*refs, width, vertical, aliased):
    if vertical:
        gm_ref, gp_ref, gn_ref, val_ref, w_ref, b_ref = refs[:6]
        rest = refs[6:]
    else:
        gm_ref, val_ref, w_ref, b_ref = refs[:4]
        rest = refs[4:]
    if aliased:
        rest = rest[1:]
    o_ref, s_scr = rest
    j = pl.program_id(1)
    nj = pl.num_programs(1)
    tl, tf = o_ref.shape
    pad = 8
    base = pad + (width if vertical else 0)
    s_scr[0:pad, :] = jnp.zeros((pad, tf), F32)
    s_scr[s_scr.shape[0] - pad:, :] = jnp.zeros((pad, tf), F32)
    s_scr[base:base + tl, :] = gm_ref[...].astype(F32)
    if vertical:
        s_scr[pad:pad + width, :] = jnp.where(j > 0, gp_ref[...].astype(F32), 0.0)
        s_scr[base + tl:base + tl + width, :] = jnp.where(j < nj - 1, gn_ref[...].astype(F32), 0.0)
    bias = b_ref[...]
    rb = 32
    dys = (-1, 0, 1) if vertical else (0,)
    for r in range(tl // rb):
        col = (lax.broadcasted_iota(jnp.int32, (rb, tf), 0) + r * rb) % width
        acc = jnp.zeros((rb, tf), F32)
        for dx in (-1, 0, 1):
            part = jnp.zeros((rb, tf), F32)
            for dy in dys:
                tap = (dy + 1) * 3 + (dx + 1)
                start = base + r * rb + dy * width + dx
                part = part + w_ref[tap:tap + 1, :] * s_scr[start:start + rb, :]
            if dx == -1:
                part = jnp.where(col != 0, part, 0.0)
            elif dx == 1:
                part = jnp.where(col != width - 1, part, 0.0)
            acc = acc + part
        conv = acc + bias
        val = val_ref[r * rb:(r + 1) * rb, :].astype(F32)
        o_ref[r * rb:(r + 1) * rb, :] = (conv * _sigmoid(conv) * val).astype(o_ref.dtype)


def _ffn_act(gv, w9, b, layer, act, *, base_row, nseq, seq, width, vertical):
    t = gv.shape[0]
    dff = w9.shape[2]
    tf = _pick(dff, (512, 256, 128))
    tl = _pick(seq, (512, 256, 128)) if vertical else seq
    assert tl % width == 0 and base_row % tl == 0 and tl % 32 == 0
    nt = seq // tl
    nf = dff // tf
    base_blk = base_row // tl
    wpt = tl // width
    last_w = t // width - 1
    main = lambda s, j, f: (base_blk + s * nt + j, f)
    in_specs = [pl.BlockSpec((tl, tf), main)]
    args = [gv]
    if vertical:
        in_specs += [
            pl.BlockSpec((width, tf), lambda s, j, f: (jnp.maximum((base_blk + s * nt + j) * wpt - 1, 0), f)),
            pl.BlockSpec((width, tf), lambda s, j, f: (jnp.minimum((base_blk + s * nt + j + 1) * wpt, last_w), f)),
        ]
        args += [gv, gv]
    in_specs += [
        pl.BlockSpec((tl, tf), lambda s, j, f: (base_blk + s * nt + j, nf + f)),
        pl.BlockSpec((None, 9, tf), lambda s, j, f: (layer, 0, f)),
        pl.BlockSpec((None, 1, tf), lambda s, j, f: (layer, 0, f)),
    ]
    args += [gv, w9, b]
    aliases = {}
    if act is not None:
        in_specs.append(pl.BlockSpec(memory_space=pl.ANY))
        args.append(act)
        aliases = {len(args) - 1: 0}
    rows_scr = tl + 16 + (2 * width if vertical else 0)
    return pl.pallas_call(
        functools.partial(_ffn_act_kernel, width=width, vertical=vertical, aliased=act is not None),
        grid=(nseq, nt, nf),
        in_specs=in_specs,
        out_specs=pl.BlockSpec((tl, tf), main),
        out_shape=jax.ShapeDtypeStruct((t, dff), BF16),
        scratch_shapes=[pltpu.VMEM((rows_scr, tf), F32)],
        input_output_aliases=aliases,
        compiler_params=_params(("arbitrary", "arbitrary", "arbitrary")),
        name="ffn_act",
    )(*args)


def _rope_tables(seq):
    t = jnp.arange(seq)
    quarter = HEAD_DIM // 4
    inv_freq = 1.0 / (ROPE_THETA ** (jnp.arange(0, quarter, dtype=F32) / quarter))
    ang_r = (t // GRID_W).astype(F32)[:, None] * inv_freq[None, :]
    ang_c = (t % GRID_W).astype(F32)[:, None] * inv_freq[None, :]
    cos = jnp.concatenate([jnp.cos(ang_r)] * 2 + [jnp.cos(ang_c)] * 2, axis=-1)
    sin = jnp.concatenate([-jnp.sin(ang_r), jnp.sin(ang_r), -jnp.sin(ang_c), jnp.sin(ang_c)], axis=-1)
    return cos, sin


def kernel(x, c, ctx, c_ctx, w_mod, b_mod, norm1_g, norm2_g, w_in, ret_decay_f, ret_decay_b, conv_dw_w, conv_dw_b,
           conv_ln_g, conv_ln_b, w_out, ffn_w_up, ffn_dw_w, ffn_dw_b, ffn_w_down, final_norm_g):
    bsz, seq, d = x.shape
    lc = ctx.shape[1]
    depth = w_mod.shape[0]
    heads = ret_decay_f.shape[1]
    rw = heads * HEAD_DIM
    cw = conv_dw_w.shape[2]
    dff = ffn_dw_b.shape[1]
    n_lat, n_ctx = bsz * seq, bsz * lc
    t = n_lat + n_ctx
    assert bsz + 1 <= MOD_ROWS and rw % cw == 0 and seq % GRID_W == 0 and w_mod.shape[2] == N_MOD * d

    tm = _pick(_gcd(seq, n_ctx), (1024, 512, 256, 128))
    tm_res = min(tm, 512)
    tn_in = _pick(2 * rw, (512, 256, 128))
    tn_up = _pick(2 * dff, (1024, 512, 256, 128))
    tk_down = _pick(dff, (1408, 1024, 512, 256, 128))

    cs = jnp.zeros((MOD_ROWS, d), F32).at[:bsz].set(c).at[bsz].set(c_ctx)
    mods = _modulation(cs, w_mod, b_mod).reshape(depth * MOD_ROWS * N_MOD, 1, d)

    x_all = jnp.concatenate([x.reshape(n_lat, d), ctx.reshape(n_ctx, d)], axis=0)
    w_in_b, w_out_b = w_in.astype(BF16), w_out.astype(BF16)
    w_up_b, w_down_b = ffn_w_up.astype(BF16), ffn_w_down.astype(BF16)
    cos, sin = _rope_tables(seq)
    dec = jnp.broadcast_to(jnp.stack([ret_decay_f, ret_decay_b], axis=1).astype(F32)[:, :, :, None, None],
                           (depth, 2, heads, 1, HEAD_DIM))
    n1 = norm1_g.reshape(depth, 1, d)
    n2 = norm2_g.reshape(depth, 1, d)
    cb = conv_dw_b.reshape(depth, 1, cw)
    clg = conv_ln_g.reshape(depth, 1, cw)
    clb = conv_ln_b.reshape(depth, 1, cw)
    w9 = ffn_dw_w.reshape(depth, 9, dff)
    fb = ffn_dw_b.reshape(depth, 1, dff)
    zero_state = jnp.zeros((bsz, heads, HEAD_DIM, HEAD_DIM), F32)

    tiles_all, tiles_lat = t // tm, n_lat // tm
    res_all, res_lat = t // tm_res, n_lat // tm_res
    mix_w = rw + cw
    col_a = 4 * rw // cw

    h = _first_norm(x_all, n1, mods, 0, tm, seq, bsz)
    out = None
    for l in range(depth):
        last = l == depth - 1
        p = _projection(h, w_in_b, l, tiles_all, tm, tn_in, rope=(cos, sin, tiles_lat, 2 * rw))
        mix, s_f, s_b = _retention(p, dec, l, zero_state, zero_state, None, base_row=n_lat, nseq=bsz, seq=lc,
                                   heads=heads, mix_width=mix_w)
        mix, _, _ = _retention(p, dec, l, s_f, s_b, mix, base_row=0, nseq=bsz, seq=seq, heads=heads, mix_width=mix_w)
        mix = _conformer_conv(p, conv_dw_w, cb, clg, clb, l, mix, base_row=0, nseq=bsz, seq=seq, col_a=col_a,
                              col_out=rw // cw)
        if not last:
            mix = _conformer_conv(p, conv_dw_w, cb, clg, clb, l, mix, base_row=n_lat, nseq=bsz, seq=lc, col_a=col_a,
                                  col_out=rw // cw)
        x_all, h2 = _mm_residual(mix, w_out_b, l, x_all, mods, 2, n2, l, (l, 3, 4),
                                 n_tiles=res_lat if last else res_all, tm=tm_res, tk=mix_w, seq=seq, batch=bsz)
        gv = _projection(h2, w_up_b, l, tiles_lat if last else tiles_all, tm, tn_up)
        act = _ffn_act(gv, w9, fb, l, None, base_row=0, nseq=bsz, seq=seq, width=GRID_W, vertical=True)
        if not last:
            act = _ffn_act(gv, w9, fb, l, act, base_row=n_lat, nseq=bsz, seq=lc, width=lc, vertical=False)
            x_all, h = _mm_residual(act, w_down_b, l, x_all, mods, 5, n1, l + 1, (l + 1, 0, 1),
                                    n_tiles=res_all, tm=tm_res, tk=tk_down, seq=seq, batch=bsz)
        else:
            out = _mm_residual(act, w_down_b, l, x_all, mods, 5, final_norm_g, 0, None,
                               n_tiles=res_lat, tm=tm_res, tk=tk_down, seq=seq, batch=bsz, final_rows=n_lat)
    return out.reshape(bsz, seq, d)


def _gcd(a, b):
    while b:
        a, b = b, a % b
    return a
```

```python
import functools

import jax
import jax.numpy as jnp
from jax import lax
from jax.experimental import pallas as pl
from jax.experimental.pallas import tpu as pltpu

F32 = jnp.float32
BF16 = jnp.bfloat16

GRID_W = 64
HEAD_DIM = 128
CHUNK = 128
N_MOD = 6
ROPE_THETA = 10000.0
EPS = 1e-6
MOD_ROWS = 8
CONV_HALO = 16
ROW_BLOCK = 16
LOOP_UNROLL = 4
VMEM_LIMIT = 52 * 1024 * 1024

ACT_DTYPE = BF16


def _pick(n, candidates):
    for c in candidates:
        if n % c == 0:
            return c
    raise ValueError(f"no tile for {n} in {candidates}")


def _params(sem):
    return pltpu.CompilerParams(dimension_semantics=sem, vmem_limit_bytes=VMEM_LIMIT)


def _sigmoid(x):
    return jax.nn.sigmoid(x)


def _norm_mod(x, g, sh, sc):
    xn = x * lax.rsqrt(jnp.mean(x * x, axis=-1, keepdims=True) + EPS)
    return (xn * g) * (1.0 + sc) + sh


def _mod_kernel(c_ref, w_ref, b_ref, o_ref):
    c = c_ref[...]
    s = c * _sigmoid(c)
    o_ref[...] = jnp.dot(s.astype(BF16), w_ref[...].astype(BF16), preferred_element_type=F32) + b_ref[...]


def _modulation(cs, w_mod, b_mod):
    depth, d, n = w_mod.shape
    tn = _pick(n, (1024, 512, 256, 128))
    return pl.pallas_call(
        _mod_kernel,
        grid=(depth, n // tn),
        in_specs=[
            pl.BlockSpec((MOD_ROWS, d), lambda l, j: (0, 0)),
            pl.BlockSpec((None, d, tn), lambda l, j: (l, 0, j)),
            pl.BlockSpec((None, 1, tn), lambda l, j: (l, 0, j)),
        ],
        out_specs=pl.BlockSpec((None, MOD_ROWS, tn), lambda l, j: (l, 0, j)),
        out_shape=jax.ShapeDtypeStruct((depth, MOD_ROWS, n), F32),
        compiler_params=_params(("arbitrary", "arbitrary")),
        name="modulation",
    )(cs, w_mod, b_mod.reshape(depth, 1, n))


def _mod_spec(layer, which, tm, seq, batch, d):
    def imap(i, *_):
        grp = jnp.minimum((i * tm) // seq, batch)
        return ((layer * MOD_ROWS + grp) * N_MOD + which, 0, 0)
    return pl.BlockSpec((None, 1, d), imap)


def _norm_mod_kernel(x_ref, g_ref, sh_ref, sc_ref, h_ref):
    g, sh, sc = g_ref[...], sh_ref[...], sc_ref[...]

    def body(r, carry):
        rows = pl.ds(pl.multiple_of(r * ROW_BLOCK, ROW_BLOCK), ROW_BLOCK)
        h_ref[rows, :] = _norm_mod(x_ref[rows, :], g, sh, sc).astype(h_ref.dtype)
        return carry

    lax.fori_loop(0, x_ref.shape[0] // ROW_BLOCK, body, 0, unroll=LOOP_UNROLL)


def _first_norm(x_all, norm_g, mods, layer, tm, seq, batch):
    t, d = x_all.shape
    return pl.pallas_call(
        _norm_mod_kernel,
        grid=(t // tm,),
        in_specs=[
            pl.BlockSpec((tm, d), lambda i: (i, 0)),
            pl.BlockSpec((None, 1, d), lambda i: (layer, 0, 0)),
            _mod_spec(layer, 0, tm, seq, batch, d),
            _mod_spec(layer, 1, tm, seq, batch, d),
        ],
        out_specs=pl.BlockSpec((tm, d), lambda i: (i, 0)),
        out_shape=jax.ShapeDtypeStruct((t, d), BF16),
        compiler_params=_params(("arbitrary",)),
        name="first_norm",
    )(x_all, norm_g, mods, mods)


def _proj_kernel(a_ref, w_ref, o_ref):
    o_ref[...] = jnp.dot(a_ref[...], w_ref[...], preferred_element_type=F32).astype(o_ref.dtype)


def _projection(a, w, layer, n_tiles, tm, tn):
    t, kdim = a.shape
    n = w.shape[2]
    return pl.pallas_call(
        _proj_kernel,
        grid=(n_tiles, n // tn),
        in_specs=[
            pl.BlockSpec((tm, kdim), lambda i, j: (i, 0)),
            pl.BlockSpec((None, kdim, tn), lambda i, j: (layer, 0, j)),
        ],
        out_specs=pl.BlockSpec((tm, tn), lambda i, j: (i, j)),
        out_shape=jax.ShapeDtypeStruct((t, n), ACT_DTYPE),
        compiler_params=_params(("arbitrary", "arbitrary")),
        name="projection",
    )(a, w)


def _log_sigmoid(x):
    return jnp.minimum(x, 0.0) - jnp.log1p(jnp.exp(-jnp.abs(x)))


def _rope(x, cos, sin_signed):
    lane = lax.broadcasted_iota(jnp.int32, x.shape, 1)
    first = (lane % (HEAD_DIM // 2)) < (HEAD_DIM // 4)
    partner = jnp.where(first, pltpu.roll(x, HEAD_DIM - HEAD_DIM // 4, 1), pltpu.roll(x, HEAD_DIM // 4, 1))
    return x * cos + partner * sin_signed


def _retention_kernel(dec_ref, q_ref, k_ref, v_ref, g_ref, s0f_ref, s0b_ref, *rest, nseg, nchunk, aliased, rope):
    if rope:
        cos_ref, sin_ref = rest[:2]
        rest = rest[2:]
    if aliased:
        rest = rest[1:]
    o_ref, sf_ref, sb_ref, of_scr, state_scr, decay_scr = rest
    s = pl.program_id(2)
    c_len = CHUNK
    lseg = nchunk * c_len
    scale = HEAD_DIM ** -0.5

    @pl.when(s == 0)
    def _init():
        row = lax.broadcasted_iota(jnp.int32, (c_len, HEAD_DIM), 0).astype(F32)
        col = lax.broadcasted_iota(jnp.int32, (c_len, HEAD_DIM), 1).astype(F32)
        for d in range(2):
            lg = _log_sigmoid(dec_ref[d])
            diff = (row - col) if d == 0 else (col - row)
            decay_scr[d, 0] = jnp.where(diff >= 0, jnp.exp(lg * jnp.maximum(diff, 0.0)), 0.0)
            if d == 0:
                decay_scr[d, 1] = jnp.exp(lg * (row + 1.0))
                decay_scr[d, 2] = jnp.exp(lg * (c_len - 1.0 - row))
            else:
                decay_scr[d, 1] = jnp.exp(lg * (c_len - row))
                decay_scr[d, 2] = jnp.exp(lg * row)
            decay_scr[d, 3] = jnp.broadcast_to(jnp.exp(lg * float(c_len)), (c_len, HEAD_DIM))
        state_scr[...] = s0f_ref[...]

    @pl.when(s == nseg)
    def _init_b():
        state_scr[...] = s0b_ref[...]

    def chunk(c, d, state):
        rows = slice(c * c_len, (c + 1) * c_len)
        qf = q_ref[rows, :].astype(F32)
        kf = k_ref[rows, :].astype(F32)
        if rope:
            cos, sin = cos_ref[rows, :], sin_ref[rows, :]
            qf, kf = _rope(qf, cos, sin), _rope(kf, cos, sin)
        q = qf.astype(BF16)
        kf = kf * scale
        v = v_ref[rows, :].astype(BF16)
        sc = lax.dot_general(q, kf.astype(BF16), (((1,), (1,)), ((), ())), preferred_element_type=F32)
        sc = sc * decay_scr[d, 0]
        o = jnp.dot(sc.astype(BF16), v, preferred_element_type=F32)
        o = o + jnp.dot(q, state.astype(BF16), preferred_element_type=F32) * decay_scr[d, 1]
        kd = (kf * decay_scr[d, 2]).astype(BF16)
        state = state * decay_scr[d, 3] + lax.dot_general(kd, v, (((0,), (0,)), ((), ())), preferred_element_type=F32)
        return o, state

    @pl.when(s < nseg)
    def _fwd():
        state = state_scr[...]
        for c in range(nchunk):
            o, state = chunk(c, 0, state)
            of_scr[pl.ds(pl.multiple_of(s * lseg + c * c_len, c_len), c_len), :] = o
        state_scr[...] = state

        @pl.when(s == nseg - 1)
        def _():
            sf_ref[...] = state

    @pl.when(s >= nseg)
    def _bwd():
        seg = 2 * nseg - 1 - s
        state = state_scr[...]
        for c in reversed(range(nchunk)):
            o, state = chunk(c, 1, state)
            o = o + of_scr[pl.ds(pl.multiple_of(seg * lseg + c * c_len, c_len), c_len), :]
            o = o * lax.rsqrt(jnp.mean(o * o, axis=-1, keepdims=True) + EPS)
            g = g_ref[c * c_len:(c + 1) * c_len, :].astype(F32)
            o_ref[c * c_len:(c + 1) * c_len, :] = (g * _sigmoid(g) * o).astype(o_ref.dtype)
        state_scr[...] = state

        @pl.when(s == 2 * nseg - 1)
        def _():
            sb_ref[...] = state


def _retention(p, dec, layer, s0f, s0b, mix, rope, *, base_row, nseq, seq, heads, mix_width):
    t = p.shape[0]
    lseg = _pick(seq, (1024, 512, 256, 128))
    nseg = seq // lseg
    nchunk = lseg // CHUNK
    base_blk = base_row // lseg
    assert base_row % lseg == 0

    def seg_of(s):
        return jnp.where(s < nseg, s, 2 * nseg - 1 - s)

    def out_seg_of(s):
        return jnp.where(s < nseg, nseg - 1, 2 * nseg - 1 - s)

    def in_spec(col0):
        return pl.BlockSpec((lseg, HEAD_DIM), lambda b, h, s: (base_blk + b * nseg + seg_of(s), col0 + h))

    state_spec = pl.BlockSpec((None, None, HEAD_DIM, HEAD_DIM), lambda b, h, s: (b, h, 0, 0))
    in_specs = [
        pl.BlockSpec((None, 2, None, 1, HEAD_DIM), lambda b, h, s: (layer, 0, h, 0, 0)),
        in_spec(0), in_spec(heads), in_spec(2 * heads),
        pl.BlockSpec((lseg, HEAD_DIM), lambda b, h, s: (base_blk + b * nseg + out_seg_of(s), 3 * heads + h)),
        state_spec, state_spec,
    ]
    args = [dec, p, p, p, p, s0f, s0b]
    if rope is not None:
        in_specs += [pl.BlockSpec((lseg, HEAD_DIM), lambda b, h, s: (seg_of(s), 0))] * 2
        args += list(rope)
    aliases = {}
    if mix is not None:
        in_specs.append(pl.BlockSpec(memory_space=pl.ANY))
        args.append(mix)
        aliases = {len(args) - 1: 0}
    state_shape = jax.ShapeDtypeStruct((nseq, heads, HEAD_DIM, HEAD_DIM), F32)
    return pl.pallas_call(
        functools.partial(_retention_kernel, nseg=nseg, nchunk=nchunk, aliased=mix is not None,
                          rope=rope is not None),
        grid=(nseq, heads, 2 * nseg),
        in_specs=in_specs,
        out_specs=[
            pl.BlockSpec((lseg, HEAD_DIM), lambda b, h, s: (base_blk + b * nseg + out_seg_of(s), h)),
            state_spec, state_spec,
        ],
        out_shape=[jax.ShapeDtypeStruct((t, mix_width), BF16), state_shape, state_shape],
        scratch_shapes=[
            pltpu.VMEM((seq, HEAD_DIM), F32),
            pltpu.VMEM((HEAD_DIM, HEAD_DIM), F32),
            pltpu.VMEM((2, 4, CHUNK, HEAD_DIM), F32),
        ],
        input_output_aliases=aliases,
        compiler_params=_params(("arbitrary", "arbitrary", "arbitrary")),
        name="retention",
    )(*args)


def _cconv_kernel(am_ref, ap_ref, an_ref, bm_ref, bp_ref, bn_ref, w_ref, b_ref, lg_ref, lb_ref, mix_ref,
                  o_ref, u_scr, y_scr, us_scr, *, taps):
    del mix_ref
    j = pl.program_id(1)
    nj = pl.num_programs(1)
    tl, cw = o_ref.shape
    halo = CONV_HALO

    def glu(a_ref, b_ref_):
        return a_ref[...].astype(F32) * _sigmoid(b_ref_[...].astype(F32))

    u_scr[halo:halo + tl, :] = glu(am_ref, bm_ref)
    u_scr[0:halo, :] = jnp.where(j > 0, glu(ap_ref, bp_ref), 0.0)
    u_scr[halo + tl:halo + tl + halo, :] = jnp.where(j < nj - 1, glu(an_ref, bn_ref), 0.0)

    off0 = halo - taps // 2
    rb = 64 if tl % 64 == 0 else tl

    n_ext = us_scr.shape[1]

    def lane_body(c, carry):
        lanes = pl.ds(pl.multiple_of(c * 128, 128), 128)
        for ph in range(1, 8):
            us_scr[ph - 1] = u_scr[ph:ph + n_ext, lanes]
        for r in range(tl // rb):
            acc = jnp.zeros((rb, 128), F32)
            for k in range(taps):
                ph, a0 = (off0 + k) % 8, r * rb + 8 * ((off0 + k) // 8)
                src = u_scr[a0:a0 + rb, lanes] if ph == 0 else us_scr[ph - 1, a0:a0 + rb, :]
                acc = acc + w_ref[k:k + 1, lanes] * src
            y_scr[r * rb:(r + 1) * rb, lanes] = acc
        return carry

    lax.fori_loop(0, cw // 128, lane_body, 0)

    bias, ln_g, ln_b = b_ref[...], lg_ref[...], lb_ref[...]

    def row_body(r, carry):
        rows = pl.ds(pl.multiple_of(r * ROW_BLOCK, ROW_BLOCK), ROW_BLOCK)
        y = y_scr[rows, :] + bias
        mu = jnp.mean(y, axis=-1, keepdims=True)
        yc = y - mu
        var = jnp.mean(yc * yc, axis=-1, keepdims=True)
        z = yc * lax.rsqrt(var + EPS) * ln_g + ln_b
        o_ref[rows, :] = (z * _sigmoid(z)).astype(o_ref.dtype)
        return carry

    lax.fori_loop(0, tl // ROW_BLOCK, row_body, 0, unroll=LOOP_UNROLL)


def _conformer_conv(p, w, b, ln_g, ln_b, layer, mix, *, base_row, nseq, seq, col_a, col_out):
    t = p.shape[0]
    taps, cw = w.shape[1], w.shape[2]
    tl = _pick(seq, (512, 256, 128))
    nt = seq // tl
    assert base_row % tl == 0 and taps // 2 < CONV_HALO
    base_blk = base_row // tl
    hpt = tl // CONV_HALO
    last_halo = t // CONV_HALO - 1

    def main_spec(col):
        return pl.BlockSpec((tl, cw), lambda s, j: (base_blk + s * nt + j, col))

    def prev_spec(col):
        return pl.BlockSpec((CONV_HALO, cw), lambda s, j: (jnp.maximum((base_blk + s * nt + j) * hpt - 1, 0), col))

    def next_spec(col):
        return pl.BlockSpec((CONV_HALO, cw), lambda s, j: (jnp.minimum((base_blk + s * nt + j + 1) * hpt, last_halo), col))

    vec_spec = pl.BlockSpec((None, 1, cw), lambda s, j: (layer, 0, 0))
    return pl.pallas_call(
        functools.partial(_cconv_kernel, taps=taps),
        grid=(nseq, nt),
        in_specs=[
            main_spec(col_a), prev_spec(col_a), next_spec(col_a),
            main_spec(col_a + 1), prev_spec(col_a + 1), next_spec(col_a + 1),
            pl.BlockSpec((None, taps, cw), lambda s, j: (layer, 0, 0)),
            vec_spec, vec_spec, vec_spec,
            pl.BlockSpec(memory_space=pl.ANY),
        ],
        out_specs=pl.BlockSpec((tl, cw), lambda s, j: (base_blk + s * nt + j, col_out)),
        out_shape=jax.ShapeDtypeStruct(mix.shape, mix.dtype),
        scratch_shapes=[pltpu.VMEM((tl + 2 * CONV_HALO, cw), F32), pltpu.VMEM((tl, cw), F32),
                        pltpu.VMEM((7, tl + 8 * ((CONV_HALO + taps // 2) // 8), 128), F32)],
        input_output_aliases={10: 0},
        compiler_params=_params(("arbitrary", "arbitrary")),
        name="conformer_conv",
    )(p, p, p, p, p, p, w, b, ln_g, ln_b, mix)


def _mm_res_kernel(a_ref, w_ref, x_ref, gate_ref, ng_ref, *rest, final):
    if final:
        o_ref, acc_scr = rest
    else:
        sh_ref, sc_ref, xo_ref, h_ref, acc_scr = rest
    k = pl.program_id(1)
    nk = pl.num_programs(1)
    prod = jnp.dot(a_ref[...], w_ref[...], preferred_element_type=F32)

    @pl.when(k == 0)
    def _():
        acc_scr[...] = prod

    @pl.when(k > 0)
    def _():
        acc_scr[...] += prod

    @pl.when(k == nk - 1)
    def _():
        gate, ng = gate_ref[...], ng_ref[...]
        if not final:
            sh, sc = sh_ref[...], sc_ref[...]

        def body(r, carry):
            rows = pl.ds(pl.multiple_of(r * ROW_BLOCK, ROW_BLOCK), ROW_BLOCK)
            xn = x_ref[rows, :] + gate * acc_scr[rows, :]
            if final:
                o_ref[rows, :] = xn * lax.rsqrt(jnp.mean(xn * xn, axis=-1, keepdims=True) + EPS) * ng
            else:
                xo_ref[rows, :] = xn
                h_ref[rows, :] = _norm_mod(xn, ng, sh, sc).astype(h_ref.dtype)
            return carry

        lax.fori_loop(0, x_ref.shape[0] // ROW_BLOCK, body, 0, unroll=LOOP_UNROLL)


def _mm_residual(a, w, layer, x_all, mods, gate_which, norm_g, norm_layer, next_mods, *, n_tiles, tm, tk, seq, batch,
                 final_rows=None):
    t, d = x_all.shape
    kdim = a.shape[1]
    final = next_mods is None
    in_specs = [
        pl.BlockSpec((tm, tk), lambda i, k: (i, k)),
        pl.BlockSpec((None, tk, d), lambda i, k: (layer, k, 0)),
        pl.BlockSpec((tm, d), lambda i, k: (i, 0)),
        _mod_spec(layer, gate_which, tm, seq, batch, d),
    ]
    args = [a, w, x_all, mods]
    if final:
        in_specs.append(pl.BlockSpec((1, d), lambda i, k: (0, 0)))
        args.append(norm_g.reshape(1, d))
        out_specs = pl.BlockSpec((tm, d), lambda i, k: (i, 0))
        out_shape = jax.ShapeDtypeStruct((final_rows, d), F32)
        aliases = {}
    else:
        nl, shw, scw = next_mods
        in_specs += [
            pl.BlockSpec((None, 1, d), lambda i, k: (norm_layer, 0, 0)),
            _mod_spec(nl, shw, tm, seq, batch, d),
            _mod_spec(nl, scw, tm, seq, batch, d),
        ]
        args += [norm_g, mods, mods]
        out_specs = [pl.BlockSpec((tm, d), lambda i, k: (i, 0)), pl.BlockSpec((tm, d), lambda i, k: (i, 0))]
        out_shape = [jax.ShapeDtypeStruct((t, d), F32), jax.ShapeDtypeStruct((t, d), BF16)]
        aliases = {2: 0}
    return pl.pallas_call(
        functools.partial(_mm_res_kernel, final=final),
        grid=(n_tiles, kdim // tk),
        in_specs=in_specs,
        out_specs=out_specs,
        out_shape=out_shape,
        scratch_shapes=[pltpu.VMEM((tm, d), F32)],
        input_output_aliases=aliases,
        compiler_params=_params(("arbitrary", "arbitrary")),
        name="matmul_residual",
    )(*args)


def _ffn_act_kernel(*refs, width, vertical, aliased):
    if vertical:
        gm_ref, gp_ref, gn_ref, val_ref, w_ref, b_ref = refs[:6]
        rest = refs[6:]
    else:
        gm_ref, val_ref, w_ref, b_ref = refs[:4]
        rest = refs[4:]
    if aliased:
        rest = rest[1:]
    o_ref, s_scr = rest
    j = pl.program_id(1)
    nj = pl.num_programs(1)
    tl, tf = o_ref.shape
    pad = 8
    base = pad + (width if vertical else 0)
    s_scr[0:pad, :] = jnp.zeros((pad, tf), F32)
    s_scr[s_scr.shape[0] - pad:, :] = jnp.zeros((pad, tf), F32)
    s_scr[base:base + tl, :] = gm_ref[...].astype(F32)
    if vertical:
        s_scr[pad:pad + width, :] = jnp.where(j > 0, gp_ref[...].astype(F32), 0.0)
        s_scr[base + tl:base + tl + width, :] = jnp.where(j < nj - 1, gn_ref[...].astype(F32), 0.0)
    bias = b_ref[...]
    rb = 32
    dys = (-1, 0, 1) if vertical else (0,)
    for r in range(tl // rb):
        col = (lax.broadcasted_iota(jnp.int32, (rb, tf), 0) + r * rb) % width
        acc = jnp.zeros((rb, tf), F32)
        for dx in (-1, 0, 1):
            part = jnp.zeros((rb, tf), F32)
            for dy in dys:
                tap = (dy + 1) * 3 + (dx + 1)
                start = base + r * rb + dy * width + dx
                part = part + w_ref[tap:tap + 1, :] * s_scr[start:start + rb, :]
            if dx == -1:
                part = jnp.where(col != 0, part, 0.0)
            elif dx == 1:
                part = jnp.where(col != width - 1, part, 0.0)
            acc = acc + part
        conv = acc + bias
        val = val_ref[r * rb:(r + 1) * rb, :].astype(F32)
        o_ref[r * rb:(r + 1) * rb, :] = (conv * _sigmoid(conv) * val).astype(o_ref.dtype)


def _ffn_act(gv, w9, b, layer, act, *, base_row, nseq, seq, width, vertical):
    t = gv.shape[0]
    dff = w9.shape[2]
    tf = _pick(dff, (512, 256, 128))
    tl = _pick(seq, (512, 256, 128)) if vertical else seq
    assert tl % width == 0 and base_row % tl == 0 and tl % 32 == 0
    nt = seq // tl
    nf = dff // tf
    base_blk = base_row // tl
    wpt = tl // width
    last_w = t // width - 1
    main = lambda s, j, f: (base_blk + s * nt + j, f)
    in_specs = [pl.BlockSpec((tl, tf), main)]
    args = [gv]
    if vertical:
        in_specs += [
            pl.BlockSpec((width, tf), lambda s, j, f: (jnp.maximum((base_blk + s * nt + j) * wpt - 1, 0), f)),
            pl.BlockSpec((width, tf), lambda s, j, f: (jnp.minimum((base_blk + s * nt + j + 1) * wpt, last_w), f)),
        ]
        args += [gv, gv]
    in_specs += [
        pl.BlockSpec((tl, tf), lambda s, j, f: (base_blk + s * nt + j, nf + f)),
        pl.BlockSpec((None, 9, tf), lambda s, j, f: (layer, 0, f)),
        pl.BlockSpec((None, 1, tf), lambda s, j, f: (layer, 0, f)),
    ]
    args += [gv, w9, b]
    aliases = {}
    if act is not None:
        in_specs.append(pl.BlockSpec(memory_space=pl.ANY))
        args.append(act)
        aliases = {len(args) - 1: 0}
    rows_scr = tl + 16 + (2 * width if vertical else 0)
    return pl.pallas_call(
        functools.partial(_ffn_act_kernel, width=width, vertical=vertical, aliased=act is not None),
        grid=(nseq, nt, nf),
        in_specs=in_specs,
        out_specs=pl.BlockSpec((tl, tf), main),
        out_shape=jax.ShapeDtypeStruct((t, dff), BF16),
        scratch_shapes=[pltpu.VMEM((rows_scr, tf), F32)],
        input_output_aliases=aliases,
        compiler_params=_params(("arbitrary", "arbitrary", "arbitrary")),
        name="ffn_act",
    )(*args)


def _rope_tables(seq):
    t = jnp.arange(seq)
    quarter = HEAD_DIM // 4
    inv_freq = 1.0 / (ROPE_THETA ** (jnp.arange(0, quarter, dtype=F32) / quarter))
    ang_r = (t // GRID_W).astype(F32)[:, None] * inv_freq[None, :]
    ang_c = (t % GRID_W).astype(F32)[:, None] * inv_freq[None, :]
    cos = jnp.concatenate([jnp.cos(ang_r)] * 2 + [jnp.cos(ang_c)] * 2, axis=-1)
    sin = jnp.concatenate([-jnp.sin(ang_r), jnp.sin(ang_r), -jnp.sin(ang_c), jnp.sin(ang_c)], axis=-1)
    return cos, sin


def kernel(x, c, ctx, c_ctx, w_mod, b_mod, norm1_g, norm2_g, w_in, ret_decay_f, ret_decay_b, conv_dw_w, conv_dw_b,
           conv_ln_g, conv_ln_b, w_out, ffn_w_up, ffn_dw_w, ffn_dw_b, ffn_w_down, final_norm_g):
    bsz, seq, d = x.shape
    lc = ctx.shape[1]
    depth = w_mod.shape[0]
    heads = ret_decay_f.shape[1]
    rw = heads * HEAD_DIM
    cw = conv_dw_w.shape[2]
    dff = ffn_dw_b.shape[1]
    n_lat, n_ctx = bsz * seq, bsz * lc
    t = n_lat + n_ctx
    assert bsz + 1 <= MOD_ROWS and rw % cw == 0 and seq % GRID_W == 0 and w_mod.shape[2] == N_MOD * d

    tm = _pick(_gcd(seq, n_ctx), (1024, 512, 256, 128))
    tm_res = min(tm, 512)
    tn_in = _pick(w_in.shape[2], (1024, 512, 256, 128))
    tn_up = _pick(2 * dff, (1024, 512, 256, 128))
    tk_down = _pick(dff, (1408, 1024, 512, 256, 128))

    cs = jnp.zeros((MOD_ROWS, d), F32).at[:bsz].set(c).at[bsz].set(c_ctx)
    mods = _modulation(cs, w_mod, b_mod).reshape(depth * MOD_ROWS * N_MOD, 1, d)

    x_all = jnp.concatenate([x.reshape(n_lat, d), ctx.reshape(n_ctx, d)], axis=0)
    w_in_b, w_out_b = w_in.astype(BF16), w_out.astype(BF16)
    w_up_b, w_down_b = ffn_w_up.astype(BF16), ffn_w_down.astype(BF16)
    cos, sin = _rope_tables(seq)
    dec = jnp.broadcast_to(jnp.stack([ret_decay_f, ret_decay_b], axis=1).astype(F32)[:, :, :, None, None],
                           (depth, 2, heads, 1, HEAD_DIM))
    n1 = norm1_g.reshape(depth, 1, d)
    n2 = norm2_g.reshape(depth, 1, d)
    cb = conv_dw_b.reshape(depth, 1, cw)
    clg = conv_ln_g.reshape(depth, 1, cw)
    clb = conv_ln_b.reshape(depth, 1, cw)
    w9 = ffn_dw_w.reshape(depth, 9, dff)
    fb = ffn_dw_b.reshape(depth, 1, dff)
    zero_state = jnp.zeros((bsz, heads, HEAD_DIM, HEAD_DIM), F32)

    tiles_all, tiles_lat = t // tm, n_lat // tm
    res_all, res_lat = t // tm_res, n_lat // tm_res
    mix_w = rw + cw
    col_a = 4 * rw // cw

    h = _first_norm(x_all, n1, mods, 0, tm, seq, bsz)
    out = None
    for l in range(depth):
        last = l == depth - 1
        p = _projection(h, w_in_b, l, tiles_all, tm, tn_in)
        mix, s_f, s_b = _retention(p, dec, l, zero_state, zero_state, None, None, base_row=n_lat, nseq=bsz, seq=lc,
                                   heads=heads, mix_width=mix_w)
        mix, _, _ = _retention(p, dec, l, s_f, s_b, mix, (cos, sin), base_row=0, nseq=bsz, seq=seq, heads=heads,
                               mix_width=mix_w)
        mix = _conformer_conv(p, conv_dw_w, cb, clg, clb, l, mix, base_row=0, nseq=bsz, seq=seq, col_a=col_a,
                              col_out=rw // cw)
        if not last:
            mix = _conformer_conv(p, conv_dw_w, cb, clg, clb, l, mix, base_row=n_lat, nseq=bsz, seq=lc, col_a=col_a,
                                  col_out=rw // cw)
        x_all, h2 = _mm_residual(mix, w_out_b, l, x_all, mods, 2, n2, l, (l, 3, 4),
                                 n_tiles=res_lat if last else res_all, tm=tm_res, tk=mix_w, seq=seq, batch=bsz)
        gv = _projection(h2, w_up_b, l, tiles_lat if last else tiles_all, tm, tn_up)
        act = _ffn_act(gv, w9, fb, l, None, base_row=0, nseq=bsz, seq=seq, width=GRID_W, vertical=True)
        if not last:
            act = _ffn_act(gv, w9, fb, l, act, base_row=n_lat, nseq=bsz, seq=lc, width=lc, vertical=False)
            x_all, h = _mm_residual(act, w_down_b, l, x_all, mods, 5, n1, l + 1, (l + 1, 0, 1),
                                    n_tiles=res_all, tm=tm_res, tk=tk_down, seq=seq, batch=bsz)
        else:
            out = _mm_residual(act, w_down_b, l, x_all, mods, 5, final_norm_g, 0, None,
                               n_tiles=res_lat, tm=tm_res, tk=tk_down, seq=seq, batch=bsz, final_rows=n_lat)
    return out.reshape(bsz, seq, d)


def _gcd(a, b):
    while b:
        a, b = b, a % b
    return a
```

```python
import functools

import jax
import jax.numpy as jnp
from jax import lax
from jax.experimental import pallas as pl
from jax.experimental.pallas import tpu as pltpu

F32 = jnp.float32
BF16 = jnp.bfloat16

GRID_W = 64
HEAD_DIM = 128
CHUNK = 128
N_MOD = 6
ROPE_THETA = 10000.0
EPS = 1e-6
MOD_ROWS = 8
CONV_HALO = 16
ACT_ROWS = 64
MXU_COLS = 256
ROW_BLOCK = 16
LOOP_UNROLL = 4
VMEM_LIMIT = 52 * 1024 * 1024

ACT_DTYPE = BF16


def _pick(n, candidates):
    for c in candidates:
        if n % c == 0:
            return c
    raise ValueError(f"no tile for {n} in {candidates}")


def _params(sem):
    return pltpu.CompilerParams(dimension_semantics=sem, vmem_limit_bytes=VMEM_LIMIT)


def _sigmoid(x):
    return jax.nn.sigmoid(x)


def _norm_mod(x, g, sh, sc):
    xn = x * lax.rsqrt(jnp.mean(x * x, axis=-1, keepdims=True) + EPS)
    return (xn * g) * (1.0 + sc) + sh


def _mod_kernel(c_ref, w_ref, b_ref, o_ref):
    c = c_ref[...]
    s = c * _sigmoid(c)
    o_ref[...] = jnp.dot(s.astype(BF16), w_ref[...].astype(BF16), preferred_element_type=F32) + b_ref[...]


def _modulation(cs, w_mod, b_mod):
    depth, d, n = w_mod.shape
    tn = _pick(n, (1024, 512, 256, 128))
    return pl.pallas_call(
        _mod_kernel,
        grid=(depth, n // tn),
        in_specs=[
            pl.BlockSpec((MOD_ROWS, d), lambda l, j: (0, 0)),
            pl.BlockSpec((None, d, tn), lambda l, j: (l, 0, j)),
            pl.BlockSpec((None, 1, tn), lambda l, j: (l, 0, j)),
        ],
        out_specs=pl.BlockSpec((None, MOD_ROWS, tn), lambda l, j: (l, 0, j)),
        out_shape=jax.ShapeDtypeStruct((depth, MOD_ROWS, n), F32),
        compiler_params=_params(("arbitrary", "arbitrary")),
        name="modulation",
    )(cs, w_mod, b_mod.reshape(depth, 1, n))


def _mod_spec(layer, which, tm, seq, batch, d, tile0=0):
    def imap(i, *_):
        grp = jnp.minimum(((i + tile0) * tm) // seq, batch)
        return ((layer * MOD_ROWS + grp) * N_MOD + which, 0, 0)
    return pl.BlockSpec((None, 1, d), imap)


def _norm_mod_kernel(x_ref, g_ref, sh_ref, sc_ref, h_ref):
    g, sh, sc = g_ref[...], sh_ref[...], sc_ref[...]

    def body(r, carry):
        rows = pl.ds(pl.multiple_of(r * ROW_BLOCK, ROW_BLOCK), ROW_BLOCK)
        h_ref[rows, :] = _norm_mod(x_ref[rows, :], g, sh, sc).astype(h_ref.dtype)
        return carry

    lax.fori_loop(0, x_ref.shape[0] // ROW_BLOCK, body, 0, unroll=LOOP_UNROLL)


def _first_norm(x_all, norm_g, mods, layer, tm, seq, batch):
    t, d = x_all.shape
    return pl.pallas_call(
        _norm_mod_kernel,
        grid=(t // tm,),
        in_specs=[
            pl.BlockSpec((tm, d), lambda i: (i, 0)),
            pl.BlockSpec((None, 1, d), lambda i: (layer, 0, 0)),
            _mod_spec(layer, 0, tm, seq, batch, d),
            _mod_spec(layer, 1, tm, seq, batch, d),
        ],
        out_specs=pl.BlockSpec((tm, d), lambda i: (i, 0)),
        out_shape=jax.ShapeDtypeStruct((t, d), BF16),
        compiler_params=_params(("arbitrary",)),
        name="first_norm",
    )(x_all, norm_g, mods, mods)


def _proj_kernel(a_ref, w_ref, o_ref):
    o_ref[...] = jnp.dot(a_ref[...], w_ref[...], preferred_element_type=F32).astype(o_ref.dtype)


def _projection(a, w, layer, n_tiles, tm, tn):
    t, kdim = a.shape
    n = w.shape[2]
    return pl.pallas_call(
        _proj_kernel,
        grid=(n_tiles, n // tn),
        in_specs=[
            pl.BlockSpec((tm, kdim), lambda i, j: (i, 0)),
            pl.BlockSpec((None, kdim, tn), lambda i, j: (layer, 0, j)),
        ],
        out_specs=pl.BlockSpec((tm, tn), lambda i, j: (i, j)),
        out_shape=jax.ShapeDtypeStruct((t, n), ACT_DTYPE),
        compiler_params=_params(("arbitrary", "arbitrary")),
        name="projection",
    )(a, w)


def _log_sigmoid(x):
    return jnp.minimum(x, 0.0) - jnp.log1p(jnp.exp(-jnp.abs(x)))


def _rope(x, cos, sin_signed):
    lane = lax.broadcasted_iota(jnp.int32, x.shape, 1)
    first = (lane % (HEAD_DIM // 2)) < (HEAD_DIM // 4)
    partner = jnp.where(first, pltpu.roll(x, HEAD_DIM - HEAD_DIM // 4, 1), pltpu.roll(x, HEAD_DIM // 4, 1))
    return x * cos + partner * sin_signed


def _retention_kernel(dec_ref, q_ref, k_ref, v_ref, g_ref, s0f_ref, s0b_ref, *rest, nseg, nchunk, aliased, rope):
    if rope:
        cos_ref, sin_ref = rest[:2]
        rest = rest[2:]
    if aliased:
        rest = rest[1:]
    o_ref, sf_ref, sb_ref, of_scr, state_scr, decay_scr = rest
    s = pl.program_id(2)
    c_len = CHUNK
    lseg = nchunk * c_len
    scale = HEAD_DIM ** -0.5

    @pl.when(s == 0)
    def _init():
        row = lax.broadcasted_iota(jnp.int32, (c_len, HEAD_DIM), 0).astype(F32)
        col = lax.broadcasted_iota(jnp.int32, (c_len, HEAD_DIM), 1).astype(F32)
        for d in range(2):
            lg = _log_sigmoid(dec_ref[d])
            diff = (row - col) if d == 0 else (col - row)
            decay_scr[d, 0] = jnp.where(diff >= 0, jnp.exp(lg * jnp.maximum(diff, 0.0)), 0.0)
            if d == 0:
                decay_scr[d, 1] = jnp.exp(lg * (row + 1.0))
                decay_scr[d, 2] = jnp.exp(lg * (c_len - 1.0 - row))
            else:
                decay_scr[d, 1] = jnp.exp(lg * (c_len - row))
                decay_scr[d, 2] = jnp.exp(lg * row)
            decay_scr[d, 3] = jnp.broadcast_to(jnp.exp(lg * float(c_len)), (c_len, HEAD_DIM))
        state_scr[...] = s0f_ref[...]

    @pl.when(s == nseg)
    def _init_b():
        state_scr[...] = s0b_ref[...]

    def chunk(c, d, state):
        rows = slice(c * c_len, (c + 1) * c_len)
        qf = q_ref[rows, :].astype(F32)
        kf = k_ref[rows, :].astype(F32)
        if rope:
            cos, sin = cos_ref[rows, :], sin_ref[rows, :]
            qf, kf = _rope(qf, cos, sin), _rope(kf, cos, sin)
        q = qf.astype(BF16)
        kf = kf * scale
        v = v_ref[rows, :].astype(BF16)
        sc = lax.dot_general(q, kf.astype(BF16), (((1,), (1,)), ((), ())), preferred_element_type=F32)
        sc = sc * decay_scr[d, 0]
        o = jnp.dot(sc.astype(BF16), v, preferred_element_type=F32)
        o = o + jnp.dot(q, state.astype(BF16), preferred_element_type=F32) * decay_scr[d, 1]
        kd = (kf * decay_scr[d, 2]).astype(BF16)
        state = state * decay_scr[d, 3] + lax.dot_general(kd, v, (((0,), (0,)), ((), ())), preferred_element_type=F32)
        return o, state

    @pl.when(s < nseg)
    def _fwd():
        state = state_scr[...]
        for c in range(nchunk):
            o, state = chunk(c, 0, state)
            of_scr[pl.ds(pl.multiple_of(s * lseg + c * c_len, c_len), c_len), :] = o
        state_scr[...] = state

        @pl.when(s == nseg - 1)
        def _():
            sf_ref[...] = state

    @pl.when(s >= nseg)
    def _bwd():
        seg = 2 * nseg - 1 - s
        state = state_scr[...]
        for c in reversed(range(nchunk)):
            o, state = chunk(c, 1, state)
            o = o + of_scr[pl.ds(pl.multiple_of(seg * lseg + c * c_len, c_len), c_len), :]
            o = o * lax.rsqrt(jnp.mean(o * o, axis=-1, keepdims=True) + EPS)
            g = g_ref[c * c_len:(c + 1) * c_len, :].astype(F32)
            o_ref[c * c_len:(c + 1) * c_len, :] = (g * _sigmoid(g) * o).astype(o_ref.dtype)
        state_scr[...] = state

        @pl.when(s == 2 * nseg - 1)
        def _():
            sb_ref[...] = state


def _retention(p, dec, layer, s0f, s0b, mix, rope, *, base_row, nseq, seq, heads, mix_width):
    t = p.shape[0]
    lseg = _pick(seq, (1024, 512, 256, 128))
    nseg = seq // lseg
    nchunk = lseg // CHUNK
    base_blk = base_row // lseg
    assert base_row % lseg == 0

    def seg_of(s):
        return jnp.where(s < nseg, s, 2 * nseg - 1 - s)

    def out_seg_of(s):
        return jnp.where(s < nseg, nseg - 1, 2 * nseg - 1 - s)

    def in_spec(col0):
        return pl.BlockSpec((lseg, HEAD_DIM), lambda b, h, s: (base_blk + b * nseg + seg_of(s), col0 + h))

    state_spec = pl.BlockSpec((None, None, HEAD_DIM, HEAD_DIM), lambda b, h, s: (b, h, 0, 0))
    in_specs = [
        pl.BlockSpec((None, 2, None, 1, HEAD_DIM), lambda b, h, s: (layer, 0, h, 0, 0)),
        in_spec(0), in_spec(heads), in_spec(2 * heads),
        pl.BlockSpec((lseg, HEAD_DIM), lambda b, h, s: (base_blk + b * nseg + out_seg_of(s), 3 * heads + h)),
        state_spec, state_spec,
    ]
    args = [dec, p, p, p, p, s0f, s0b]
    if rope is not None:
        in_specs += [pl.BlockSpec((lseg, HEAD_DIM), lambda b, h, s: (seg_of(s), 0))] * 2
        args += list(rope)
    aliases = {}
    if mix is not None:
        in_specs.append(pl.BlockSpec(memory_space=pl.ANY))
        args.append(mix)
        aliases = {len(args) - 1: 0}
    state_shape = jax.ShapeDtypeStruct((nseq, heads, HEAD_DIM, HEAD_DIM), F32)
    return pl.pallas_call(
        functools.partial(_retention_kernel, nseg=nseg, nchunk=nchunk, aliased=mix is not None,
                          rope=rope is not None),
        grid=(nseq, heads, 2 * nseg),
        in_specs=in_specs,
        out_specs=[
            pl.BlockSpec((lseg, HEAD_DIM), lambda b, h, s: (base_blk + b * nseg + out_seg_of(s), h)),
            state_spec, state_spec,
        ],
        out_shape=[jax.ShapeDtypeStruct((t, mix_width), BF16), state_shape, state_shape],
        scratch_shapes=[
            pltpu.VMEM((seq, HEAD_DIM), F32),
            pltpu.VMEM((HEAD_DIM, HEAD_DIM), F32),
            pltpu.VMEM((2, 4, CHUNK, HEAD_DIM), F32),
        ],
        input_output_aliases=aliases,
        compiler_params=_params(("arbitrary", "arbitrary", "arbitrary")),
        name="retention",
    )(*args)


def _cconv_kernel(am_ref, ap_ref, an_ref, bm_ref, bp_ref, bn_ref, w_ref, b_ref, lg_ref, lb_ref, mix_ref,
                  o_ref, u_scr, y_scr, us_scr, *, taps):
    del mix_ref
    j = pl.program_id(1)
    nj = pl.num_programs(1)
    tl, cw = o_ref.shape
    halo = CONV_HALO

    def glu(a_ref, b_ref_):
        return a_ref[...].astype(F32) * _sigmoid(b_ref_[...].astype(F32))

    u_scr[halo:halo + tl, :] = glu(am_ref, bm_ref)
    u_scr[0:halo, :] = jnp.where(j > 0, glu(ap_ref, bp_ref), 0.0)
    u_scr[halo + tl:halo + tl + halo, :] = jnp.where(j < nj - 1, glu(an_ref, bn_ref), 0.0)

    off0 = halo - taps // 2
    rb = 64 if tl % 64 == 0 else tl

    n_ext = us_scr.shape[1]

    def lane_body(c, carry):
        lanes = pl.ds(pl.multiple_of(c * 128, 128), 128)
        for ph in range(1, 8):
            us_scr[ph - 1] = u_scr[ph:ph + n_ext, lanes]
        for r in range(tl // rb):
            acc = jnp.zeros((rb, 128), F32)
            for k in range(taps):
                ph, a0 = (off0 + k) % 8, r * rb + 8 * ((off0 + k) // 8)
                src = u_scr[a0:a0 + rb, lanes] if ph == 0 else us_scr[ph - 1, a0:a0 + rb, :]
                acc = acc + w_ref[k:k + 1, lanes] * src
            y_scr[r * rb:(r + 1) * rb, lanes] = acc
        return carry

    lax.fori_loop(0, cw // 128, lane_body, 0)

    bias, ln_g, ln_b = b_ref[...], lg_ref[...], lb_ref[...]

    def row_body(r, carry):
        rows = pl.ds(pl.multiple_of(r * ROW_BLOCK, ROW_BLOCK), ROW_BLOCK)
        y = y_scr[rows, :] + bias
        mu = jnp.mean(y, axis=-1, keepdims=True)
        yc = y - mu
        var = jnp.mean(yc * yc, axis=-1, keepdims=True)
        z = yc * lax.rsqrt(var + EPS) * ln_g + ln_b
        o_ref[rows, :] = (z * _sigmoid(z)).astype(o_ref.dtype)
        return carry

    lax.fori_loop(0, tl // ROW_BLOCK, row_body, 0, unroll=LOOP_UNROLL)


def _conformer_conv(p, w, b, ln_g, ln_b, layer, mix, *, base_row, nseq, seq, col_a, col_out):
    t = p.shape[0]
    taps, cw = w.shape[1], w.shape[2]
    tl = _pick(seq, (512, 256, 128))
    nt = seq // tl
    assert base_row % tl == 0 and taps // 2 < CONV_HALO
    base_blk = base_row // tl
    hpt = tl // CONV_HALO
    last_halo = t // CONV_HALO - 1

    def main_spec(col):
        return pl.BlockSpec((tl, cw), lambda s, j: (base_blk + s * nt + j, col))

    def prev_spec(col):
        return pl.BlockSpec((CONV_HALO, cw), lambda s, j: (jnp.maximum((base_blk + s * nt + j) * hpt - 1, 0), col))

    def next_spec(col):
        return pl.BlockSpec((CONV_HALO, cw), lambda s, j: (jnp.minimum((base_blk + s * nt + j + 1) * hpt, last_halo), col))

    vec_spec = pl.BlockSpec((None, 1, cw), lambda s, j: (layer, 0, 0))
    return pl.pallas_call(
        functools.partial(_cconv_kernel, taps=taps),
        grid=(nseq, nt),
        in_specs=[
            main_spec(col_a), prev_spec(col_a), next_spec(col_a),
            main_spec(col_a + 1), prev_spec(col_a + 1), next_spec(col_a + 1),
            pl.BlockSpec((None, taps, cw), lambda s, j: (layer, 0, 0)),
            vec_spec, vec_spec, vec_spec,
            pl.BlockSpec(memory_space=pl.ANY),
        ],
        out_specs=pl.BlockSpec((tl, cw), lambda s, j: (base_blk + s * nt + j, col_out)),
        out_shape=jax.ShapeDtypeStruct(mix.shape, mix.dtype),
        scratch_shapes=[pltpu.VMEM((tl + 2 * CONV_HALO, cw), F32), pltpu.VMEM((tl, cw), F32),
                        pltpu.VMEM((7, tl + 8 * ((CONV_HALO + taps // 2) // 8), 128), F32)],
        input_output_aliases={10: 0},
        compiler_params=_params(("arbitrary", "arbitrary")),
        name="conformer_conv",
    )(p, p, p, p, p, p, w, b, ln_g, ln_b, mix)


def _residual_epilogue(x_ref, acc_scr, gate_ref, ng_ref, sh_ref, sc_ref, xo_ref, h_ref, o_ref):
    final = o_ref is not None
    gate, ng = gate_ref[...], ng_ref[...]
    if not final:
        sh, sc = sh_ref[...], sc_ref[...]

    def body(r, carry):
        rows = pl.ds(pl.multiple_of(r * ROW_BLOCK, ROW_BLOCK), ROW_BLOCK)
        xn = x_ref[rows, :] + gate * acc_scr[rows, :]
        if final:
            o_ref[rows, :] = xn * lax.rsqrt(jnp.mean(xn * xn, axis=-1, keepdims=True) + EPS) * ng
        else:
            xo_ref[rows, :] = xn
            h_ref[rows, :] = _norm_mod(xn, ng, sh, sc).astype(h_ref.dtype)
        return carry

    lax.fori_loop(0, x_ref.shape[0] // ROW_BLOCK, body, 0, unroll=LOOP_UNROLL)


def _mm_res_kernel(a_ref, w_ref, x_ref, gate_ref, ng_ref, *rest, final, h_aliased):
    if final:
        o_ref, acc_scr = rest
    else:
        sh_ref, sc_ref = rest[:2]
        xo_ref, h_ref, acc_scr = rest[3:] if h_aliased else rest[2:]
    k = pl.program_id(1)
    nk = pl.num_programs(1)
    prod = jnp.dot(a_ref[...], w_ref[...], preferred_element_type=F32)

    @pl.when(k == 0)
    def _():
        acc_scr[...] = prod

    @pl.when(k > 0)
    def _():
        acc_scr[...] += prod

    @pl.when(k == nk - 1)
    def _():
        if final:
            _residual_epilogue(x_ref, acc_scr, gate_ref, ng_ref, None, None, None, None, o_ref)
        else:
            _residual_epilogue(x_ref, acc_scr, gate_ref, ng_ref, sh_ref, sc_ref, xo_ref, h_ref, None)


def _mm_residual(a, w, layer, x_all, mods, gate_which, norm_g, norm_layer, next_mods, *, n_tiles, tm, tk, seq, batch,
                 final_rows=None, tile0=0, h_prev=None):
    t, d = x_all.shape
    kdim = a.shape[1]
    final = next_mods is None
    in_specs = [
        pl.BlockSpec((tm, tk), lambda i, k: (i + tile0, k)),
        pl.BlockSpec((None, tk, d), lambda i, k: (layer, k, 0)),
        pl.BlockSpec((tm, d), lambda i, k: (i + tile0, 0)),
        _mod_spec(layer, gate_which, tm, seq, batch, d, tile0),
    ]
    args = [a, w, x_all, mods]
    if final:
        in_specs.append(pl.BlockSpec((1, d), lambda i, k: (0, 0)))
        args.append(norm_g.reshape(1, d))
        out_specs = pl.BlockSpec((tm, d), lambda i, k: (i + tile0, 0))
        out_shape = jax.ShapeDtypeStruct((final_rows, d), F32)
        aliases = {}
    else:
        nl, shw, scw = next_mods
        in_specs += [
            pl.BlockSpec((None, 1, d), lambda i, k: (norm_layer, 0, 0)),
            _mod_spec(nl, shw, tm, seq, batch, d, tile0),
            _mod_spec(nl, scw, tm, seq, batch, d, tile0),
        ]
        args += [norm_g, mods, mods]
        row_spec = pl.BlockSpec((tm, d), lambda i, k: (i + tile0, 0))
        out_specs = [row_spec, row_spec]
        out_shape = [jax.ShapeDtypeStruct((t, d), F32), jax.ShapeDtypeStruct((t, d), BF16)]
        aliases = {2: 0}
        if h_prev is not None:
            in_specs.append(pl.BlockSpec(memory_space=pl.ANY))
            args.append(h_prev)
            aliases[len(args) - 1] = 1
    return pl.pallas_call(
        functools.partial(_mm_res_kernel, final=final, h_aliased=h_prev is not None),
        grid=(n_tiles, kdim // tk),
        in_specs=in_specs,
        out_specs=out_specs,
        out_shape=out_shape,
        scratch_shapes=[pltpu.VMEM((tm, d), F32)],
        input_output_aliases=aliases,
        compiler_params=_params(("arbitrary", "arbitrary")),
        name="matmul_residual",
    )(*args)


def _ffn_act_kernel(*refs, width, vertical, aliased):
    if vertical:
        gm_ref, gp_ref, gn_ref, val_ref, w_ref, b_ref = refs[:6]
        rest = refs[6:]
    else:
        gm_ref, val_ref, w_ref, b_ref = refs[:4]
        rest = refs[4:]
    if aliased:
        rest = rest[1:]
    o_ref, s_scr = rest
    j = pl.program_id(1)
    if not vertical:
        gp_ref = gn_ref = None

    _act_fill(gm_ref, gp_ref, gn_ref, s_scr, j > 0, j < pl.num_programs(1) - 1, width=width, vertical=vertical)
    for r in range(o_ref.shape[0] // ACT_ROWS):
        for c in range(o_ref.shape[1] // 128):
            o_ref[r * ACT_ROWS:(r + 1) * ACT_ROWS, c * 128:(c + 1) * 128] = _act_rows(
                r, c, val_ref, w_ref, b_ref, s_scr, width=width, vertical=vertical).astype(o_ref.dtype)


def _act_fill(gm_ref, gp_ref, gn_ref, s_scr, has_above, has_below, *, width, vertical):
    tl, tf = gm_ref.shape
    pad = 8
    base = pad + (width if vertical else 0)
    s_scr[0:pad, :] = jnp.zeros((pad, tf), F32)
    s_scr[s_scr.shape[0] - pad:, :] = jnp.zeros((pad, tf), F32)
    s_scr[base:base + tl, :] = gm_ref[...].astype(F32)
    if vertical:
        s_scr[pad:pad + width, :] = jnp.where(has_above, gp_ref[...].astype(F32), 0.0)
        s_scr[base + tl:base + tl + width, :] = jnp.where(has_below, gn_ref[...].astype(F32), 0.0)


def _act_rows(r, c, val_ref, w_ref, b_ref, s_scr, *, width, vertical):
    rb, tf = ACT_ROWS, 128
    lanes = slice(c * 128, (c + 1) * 128)
    base = 8 + (width if vertical else 0)
    dys = (-1, 0, 1) if vertical else (0,)
    col = (lax.broadcasted_iota(jnp.int32, (rb, tf), 0) + r * rb) % width
    acc = jnp.zeros((rb, tf), F32)
    for dx in (-1, 0, 1):
        part = jnp.zeros((rb, tf), F32)
        for dy in dys:
            tap = (dy + 1) * 3 + (dx + 1)
            start = base + r * rb + dy * width + dx
            part = part + w_ref[tap:tap + 1, lanes] * s_scr[start:start + rb, lanes]
        if dx == -1:
            part = jnp.where(col != 0, part, 0.0)
        elif dx == 1:
            part = jnp.where(col != width - 1, part, 0.0)
        acc = acc + part
    conv = acc + b_ref[:, lanes]
    val = val_ref[r * rb:(r + 1) * rb, lanes].astype(F32)
    return conv * _sigmoid(conv) * val


def _ffn_act(gv, w9, b, layer, act, *, base_row, nseq, seq, width, vertical):
    t = gv.shape[0]
    dff = w9.shape[2]
    tf = _pick(dff, (512, 256, 128))
    tl = _pick(seq, (512, 256, 128)) if vertical else seq
    assert tl % width == 0 and base_row % tl == 0 and tl % ACT_ROWS == 0
    nt = seq // tl
    nf = dff // tf
    base_blk = base_row // tl
    wpt = tl // width
    last_w = t // width - 1
    main = lambda s, j, f: (base_blk + s * nt + j, f)
    in_specs = [pl.BlockSpec((tl, tf), main)]
    args = [gv]
    if vertical:
        in_specs += [
            pl.BlockSpec((width, tf), lambda s, j, f: (jnp.maximum((base_blk + s * nt + j) * wpt - 1, 0), f)),
            pl.BlockSpec((width, tf), lambda s, j, f: (jnp.minimum((base_blk + s * nt + j + 1) * wpt, last_w), f)),
        ]
        args += [gv, gv]
    in_specs += [
        pl.BlockSpec((tl, tf), lambda s, j, f: (base_blk + s * nt + j, nf + f)),
        pl.BlockSpec((None, 9, tf), lambda s, j, f: (layer, 0, f)),
        pl.BlockSpec((None, 1, tf), lambda s, j, f: (layer, 0, f)),
    ]
    args += [gv, w9, b]
    aliases = {}
    if act is not None:
        in_specs.append(pl.BlockSpec(memory_space=pl.ANY))
        args.append(act)
        aliases = {len(args) - 1: 0}
    rows_scr = tl + 16 + (2 * width if vertical else 0)
    return pl.pallas_call(
        functools.partial(_ffn_act_kernel, width=width, vertical=vertical, aliased=act is not None),
        grid=(nseq, nt, nf),
        in_specs=in_specs,
        out_specs=pl.BlockSpec((tl, tf), main),
        out_shape=jax.ShapeDtypeStruct((t, dff), BF16),
        scratch_shapes=[pltpu.VMEM((rows_scr, tf), F32)],
        input_output_aliases=aliases,
        compiler_params=_params(("arbitrary", "arbitrary", "arbitrary")),
        name="ffn_act",
    )(*args)


def _ffn_down_kernel(gm_ref, gp_ref, gn_ref, val_ref, w9_ref, b9_ref, wd_ref, x_ref, gate_ref, ng_ref, *rest,
                     nk, tiles_per_image, final):
    if final:
        o_ref, s_scr, act_a, act_b, acc_scr = rest
    else:
        sh_ref, sc_ref, xo_ref, h_ref, s_scr, act_a, act_b, acc_scr = rest
    k = pl.program_id(1)
    j = pl.program_id(0) % tiles_per_image
    slots = (act_a, act_b)

    n_rows = gm_ref.shape[0] // ACT_ROWS
    n_cols = acc_scr.shape[1] // MXU_COLS

    def step(produce_slot, consume_slot, first):
        n_lane = gm_ref.shape[1] // 128

        def consume(n):
            cols = slice(n * MXU_COLS, (n + 1) * MXU_COLS)
            prod = jnp.dot(slots[consume_slot][...], wd_ref[:, cols], preferred_element_type=F32)
            if first:
                acc_scr[:, cols] = prod
            else:
                acc_scr[:, cols] += prod

        def produce(r, c):
            slots[produce_slot][r * ACT_ROWS:(r + 1) * ACT_ROWS, c * 128:(c + 1) * 128] = _act_rows(
                r, c, val_ref, w9_ref, b9_ref, s_scr, width=GRID_W, vertical=True).astype(BF16)

        pieces = [(r, c) for r in range(n_rows) for c in range(n_lane)] if produce_slot is not None else []
        n_consume = n_cols if consume_slot is not None else 0
        done = 0
        if n_consume:
            consume(0)
            done = 1
        if produce_slot is not None:
            _act_fill(gm_ref, gp_ref, gn_ref, s_scr, j > 0, j < tiles_per_image - 1, width=GRID_W, vertical=True)
        for idx, (r, c) in enumerate(pieces):
            produce(r, c)
            target = 1 + (idx + 1) * (n_consume - 1) // len(pieces) if n_consume else 0
            while done < target:
                consume(done)
                done += 1
        while done < n_consume:
            consume(done)
            done += 1

    @pl.when(k == 0)
    def _():
        step(0, None, False)

    @pl.when(k == 1)
    def _():
        step(1, 0, True)

    middle = jnp.logical_and(k > 1, k < nk)

    @pl.when(jnp.logical_and(middle, k % 2 == 0))
    def _():
        step(0, 1, False)

    @pl.when(jnp.logical_and(middle, k % 2 == 1))
    def _():
        step(1, 0, False)

    @pl.when(k == nk)
    def _():
        step(None, (nk - 1) % 2, False)
        if final:
            _residual_epilogue(x_ref, acc_scr, gate_ref, ng_ref, None, None, None, None, o_ref)
        else:
            _residual_epilogue(x_ref, acc_scr, gate_ref, ng_ref, sh_ref, sc_ref, xo_ref, h_ref, None)


def _ffn_down_fused(gv, w9, b9, w_down, layer, x_all, mods, gate_which, norm_g, norm_layer, next_mods, *, n_lat, seq,
                    batch):
    t, d = x_all.shape
    dff = w9.shape[2]
    tk = _pick(dff, (512, 256, 128))
    tl = _pick(seq, (512, 256, 128))
    nk = dff // tk
    assert nk >= 2 and tl % GRID_W == 0
    wpt = tl // GRID_W
    last_w = t // GRID_W - 1
    final = next_mods is None
    kk = lambda k: jnp.minimum(k, nk - 1)
    in_specs = [
        pl.BlockSpec((tl, tk), lambda i, k: (i, kk(k))),
        pl.BlockSpec((GRID_W, tk), lambda i, k: (jnp.maximum(i * wpt - 1, 0), kk(k))),
        pl.BlockSpec((GRID_W, tk), lambda i, k: (jnp.minimum((i + 1) * wpt, last_w), kk(k))),
        pl.BlockSpec((tl, tk), lambda i, k: (i, nk + kk(k))),
        pl.BlockSpec((None, 9, tk), lambda i, k: (layer, 0, kk(k))),
        pl.BlockSpec((None, 1, tk), lambda i, k: (layer, 0, kk(k))),
        pl.BlockSpec((None, tk, d), lambda i, k: (layer, jnp.maximum(k - 1, 0), 0)),
        pl.BlockSpec((tl, d), lambda i, k: (i, 0)),
        _mod_spec(layer, gate_which, tl, seq, batch, d),
    ]
    args = [gv, gv, gv, gv, w9, b9, w_down, x_all, mods]
    row_spec = pl.BlockSpec((tl, d), lambda i, k: (i, 0))
    if final:
        in_specs.append(pl.BlockSpec((1, d), lambda i, k: (0, 0)))
        args.append(norm_g.reshape(1, d))
        out_specs, out_shape, aliases = row_spec, jax.ShapeDtypeStruct((n_lat, d), F32), {}
    else:
        nl, shw, scw = next_mods
        in_specs += [
            pl.BlockSpec((None, 1, d), lambda i, k: (norm_layer, 0, 0)),
            _mod_spec(nl, shw, tl, seq, batch, d),
            _mod_spec(nl, scw, tl, seq, batch, d),
        ]
        args += [norm_g, mods, mods]
        out_specs = [row_spec, row_spec]
        out_shape = [jax.ShapeDtypeStruct((t, d), F32), jax.ShapeDtypeStruct((t, d), BF16)]
        aliases = {7: 0}
    return pl.pallas_call(
        functools.partial(_ffn_down_kernel, nk=nk, tiles_per_image=seq // tl, final=final),
        grid=(n_lat // tl, nk + 1),
        in_specs=in_specs,
        out_specs=out_specs,
        out_shape=out_shape,
        scratch_shapes=[
            pltpu.VMEM((tl + 2 * GRID_W + 16, tk), F32),
            pltpu.VMEM((tl, tk), BF16),
            pltpu.VMEM((tl, tk), BF16),
            pltpu.VMEM((tl, d), F32),
        ],
        input_output_aliases=aliases,
        compiler_params=_params(("arbitrary", "arbitrary")),
        name="ffn_down_fused",
    )(*args)


def _rope_tables(seq):
    t = jnp.arange(seq)
    quarter = HEAD_DIM // 4
    inv_freq = 1.0 / (ROPE_THETA ** (jnp.arange(0, quarter, dtype=F32) / quarter))
    ang_r = (t // GRID_W).astype(F32)[:, None] * inv_freq[None, :]
    ang_c = (t % GRID_W).astype(F32)[:, None] * inv_freq[None, :]
    cos = jnp.concatenate([jnp.cos(ang_r)] * 2 + [jnp.cos(ang_c)] * 2, axis=-1)
    sin = jnp.concatenate([-jnp.sin(ang_r), jnp.sin(ang_r), -jnp.sin(ang_c), jnp.sin(ang_c)], axis=-1)
    return cos, sin


def kernel(x, c, ctx, c_ctx, w_mod, b_mod, norm1_g, norm2_g, w_in, ret_decay_f, ret_decay_b, conv_dw_w, conv_dw_b,
           conv_ln_g, conv_ln_b, w_out, ffn_w_up, ffn_dw_w, ffn_dw_b, ffn_w_down, final_norm_g):
    bsz, seq, d = x.shape
    lc = ctx.shape[1]
    depth = w_mod.shape[0]
    heads = ret_decay_f.shape[1]
    rw = heads * HEAD_DIM
    cw = conv_dw_w.shape[2]
    dff = ffn_dw_b.shape[1]
    n_lat, n_ctx = bsz * seq, bsz * lc
    t = n_lat + n_ctx
    assert bsz + 1 <= MOD_ROWS and rw % cw == 0 and seq % GRID_W == 0 and w_mod.shape[2] == N_MOD * d

    tm = _pick(_gcd(seq, n_ctx), (1024, 512, 256, 128))
    tm_res = min(tm, 512)
    tn_in = _pick(w_in.shape[2], (1024, 512, 256, 128))
    tn_up = _pick(2 * dff, (1024, 512, 256, 128))
    tk_down = _pick(dff, (1408, 1024, 512, 256, 128))

    cs = jnp.zeros((MOD_ROWS, d), F32).at[:bsz].set(c).at[bsz].set(c_ctx)
    mods = _modulation(cs, w_mod, b_mod).reshape(depth * MOD_ROWS * N_MOD, 1, d)

    x_all = jnp.concatenate([x.reshape(n_lat, d), ctx.reshape(n_ctx, d)], axis=0)
    w_in_b, w_out_b = w_in.astype(BF16), w_out.astype(BF16)
    w_up_b, w_down_b = ffn_w_up.astype(BF16), ffn_w_down.astype(BF16)
    cos, sin = _rope_tables(seq)
    dec = jnp.broadcast_to(jnp.stack([ret_decay_f, ret_decay_b], axis=1).astype(F32)[:, :, :, None, None],
                           (depth, 2, heads, 1, HEAD_DIM))
    n1 = norm1_g.reshape(depth, 1, d)
    n2 = norm2_g.reshape(depth, 1, d)
    cb = conv_dw_b.reshape(depth, 1, cw)
    clg = conv_ln_g.reshape(depth, 1, cw)
    clb = conv_ln_b.reshape(depth, 1, cw)
    w9 = ffn_dw_w.reshape(depth, 9, dff)
    fb = ffn_dw_b.reshape(depth, 1, dff)
    zero_state = jnp.zeros((bsz, heads, HEAD_DIM, HEAD_DIM), F32)

    tiles_all, tiles_lat = t // tm, n_lat // tm
    res_all, res_lat = t // tm_res, n_lat // tm_res
    mix_w = rw + cw
    col_a = 4 * rw // cw

    h = _first_norm(x_all, n1, mods, 0, tm, seq, bsz)
    out = None
    for l in range(depth):
        last = l == depth - 1
        p = _projection(h, w_in_b, l, tiles_all, tm, tn_in)
        mix, s_f, s_b = _retention(p, dec, l, zero_state, zero_state, None, None, base_row=n_lat, nseq=bsz, seq=lc,
                                   heads=heads, mix_width=mix_w)
        mix, _, _ = _retention(p, dec, l, s_f, s_b, mix, (cos, sin), base_row=0, nseq=bsz, seq=seq, heads=heads,
                               mix_width=mix_w)
        mix = _conformer_conv(p, conv_dw_w, cb, clg, clb, l, mix, base_row=0, nseq=bsz, seq=seq, col_a=col_a,
                              col_out=rw // cw)
        if not last:
            mix = _conformer_conv(p, conv_dw_w, cb, clg, clb, l, mix, base_row=n_lat, nseq=bsz, seq=lc, col_a=col_a,
                                  col_out=rw // cw)
        x_all, h2 = _mm_residual(mix, w_out_b, l, x_all, mods, 2, n2, l, (l, 3, 4),
                                 n_tiles=res_lat if last else res_all, tm=tm_res, tk=mix_w, seq=seq, batch=bsz)
        gv = _projection(h2, w_up_b, l, tiles_lat if last else tiles_all, tm, tn_up)
        if not last:
            act_ctx = _ffn_act(gv, w9, fb, l, None, base_row=n_lat, nseq=bsz, seq=lc, width=lc, vertical=False)
            x_all, h = _ffn_down_fused(gv, w9, fb, w_down_b, l, x_all, mods, 5, n1, l + 1, (l + 1, 0, 1),
                                       n_lat=n_lat, seq=seq, batch=bsz)
            x_all, h = _mm_residual(act_ctx, w_down_b, l, x_all, mods, 5, n1, l + 1, (l + 1, 0, 1),
                                    n_tiles=res_all - res_lat, tm=tm_res, tk=tk_down, seq=seq, batch=bsz,
                                    tile0=res_lat, h_prev=h)
        else:
            out = _ffn_down_fused(gv, w9, fb, w_down_b, l, x_all, mods, 5, final_norm_g, 0, None,
                                  n_lat=n_lat, seq=seq, batch=bsz)
    return out.reshape(bsz, seq, d)


def _gcd(a, b):
    while b:
        a, b = b, a % b
    return a
```

```python
import functools

import jax
import jax.numpy as jnp
from jax import lax
from jax.experimental import pallas as pl
from jax.experimental.pallas import tpu as pltpu

F32 = jnp.float32
BF16 = jnp.bfloat16

GRID_W = 64
HEAD_DIM = 128
CHUNK = 128
N_MOD = 6
ROPE_THETA = 10000.0
EPS = 1e-6
MOD_ROWS = 8
CONV_HALO = 16
ACT_ROWS = 64
MXU_COLS = 256
MXU_K = 512
ROW_BLOCK = 16
LOOP_UNROLL = 4
VMEM_LIMIT = 52 * 1024 * 1024

ACT_DTYPE = BF16


def _pick(n, candidates):
    for c in candidates:
        if n % c == 0:
            return c
    raise ValueError(f"no tile for {n} in {candidates}")


def _params(sem):
    return pltpu.CompilerParams(dimension_semantics=sem, vmem_limit_bytes=VMEM_LIMIT)


def _sigmoid(x):
    return jax.nn.sigmoid(x)


def _norm_mod(x, g, sh, sc):
    xn = x * lax.rsqrt(jnp.mean(x * x, axis=-1, keepdims=True) + EPS)
    return (xn * g) * (1.0 + sc) + sh


def _mod_kernel(c_ref, w_ref, b_ref, o_ref):
    c = c_ref[...]
    s = c * _sigmoid(c)
    o_ref[...] = jnp.dot(s.astype(BF16), w_ref[...].astype(BF16), preferred_element_type=F32) + b_ref[...]


def _modulation(cs, w_mod, b_mod):
    depth, d, n = w_mod.shape
    tn = _pick(n, (1024, 512, 256, 128))
    return pl.pallas_call(
        _mod_kernel,
        grid=(depth, n // tn),
        in_specs=[
            pl.BlockSpec((MOD_ROWS, d), lambda l, j: (0, 0)),
            pl.BlockSpec((None, d, tn), lambda l, j: (l, 0, j)),
            pl.BlockSpec((None, 1, tn), lambda l, j: (l, 0, j)),
        ],
        out_specs=pl.BlockSpec((None, MOD_ROWS, tn), lambda l, j: (l, 0, j)),
        out_shape=jax.ShapeDtypeStruct((depth, MOD_ROWS, n), F32),
        compiler_params=_params(("arbitrary", "arbitrary")),
        name="modulation",
    )(cs, w_mod, b_mod.reshape(depth, 1, n))


def _mod_spec(layer, which, tm, seq, batch, d, tile0=0, tile_of=None):
    def imap(i, *rest):
        tile = (i + tile0) if tile_of is None else tile_of(i, *rest)
        grp = jnp.minimum((tile * tm) // seq, batch)
        return ((layer * MOD_ROWS + grp) * N_MOD + which, 0, 0)
    return pl.BlockSpec((None, 1, d), imap)


def _norm_mod_kernel(x_ref, g_ref, sh_ref, sc_ref, h_ref):
    g, sh, sc = g_ref[...], sh_ref[...], sc_ref[...]

    def body(r, carry):
        rows = pl.ds(pl.multiple_of(r * ROW_BLOCK, ROW_BLOCK), ROW_BLOCK)
        h_ref[rows, :] = _norm_mod(x_ref[rows, :], g, sh, sc).astype(h_ref.dtype)
        return carry

    lax.fori_loop(0, x_ref.shape[0] // ROW_BLOCK, body, 0, unroll=LOOP_UNROLL)


def _first_norm(x_all, norm_g, mods, layer, tm, seq, batch):
    t, d = x_all.shape
    return pl.pallas_call(
        _norm_mod_kernel,
        grid=(t // tm,),
        in_specs=[
            pl.BlockSpec((tm, d), lambda i: (i, 0)),
            pl.BlockSpec((None, 1, d), lambda i: (layer, 0, 0)),
            _mod_spec(layer, 0, tm, seq, batch, d),
            _mod_spec(layer, 1, tm, seq, batch, d),
        ],
        out_specs=pl.BlockSpec((tm, d), lambda i: (i, 0)),
        out_shape=jax.ShapeDtypeStruct((t, d), BF16),
        compiler_params=_params(("arbitrary",)),
        name="first_norm",
    )(x_all, norm_g, mods, mods)


def _proj_kernel(a_ref, w_ref, o_ref):
    o_ref[...] = jnp.dot(a_ref[...], w_ref[...], preferred_element_type=F32).astype(o_ref.dtype)


def _projection(a, w, layer, n_tiles, tm, tn):
    t, kdim = a.shape
    n = w.shape[2]
    return pl.pallas_call(
        _proj_kernel,
        grid=(n_tiles, n // tn),
        in_specs=[
            pl.BlockSpec((tm, kdim), lambda i, j: (i, 0)),
            pl.BlockSpec((None, kdim, tn), lambda i, j: (layer, 0, j)),
        ],
        out_specs=pl.BlockSpec((tm, tn), lambda i, j: (i, j)),
        out_shape=jax.ShapeDtypeStruct((t, n), ACT_DTYPE),
        compiler_params=_params(("arbitrary", "arbitrary")),
        name="projection",
    )(a, w)


def _log_sigmoid(x):
    return jnp.minimum(x, 0.0) - jnp.log1p(jnp.exp(-jnp.abs(x)))


def _rope(x, cos, sin_signed):
    lane = lax.broadcasted_iota(jnp.int32, x.shape, 1)
    first = (lane % (HEAD_DIM // 2)) < (HEAD_DIM // 4)
    partner = jnp.where(first, pltpu.roll(x, HEAD_DIM - HEAD_DIM // 4, 1), pltpu.roll(x, HEAD_DIM // 4, 1))
    return x * cos + partner * sin_signed


def _retention_kernel(dec_ref, q_ref, k_ref, v_ref, g_ref, s0f_ref, s0b_ref, *rest, nseg, nchunk, aliased, rope):
    if rope:
        cos_ref, sin_ref = rest[:2]
        rest = rest[2:]
    if aliased:
        rest = rest[1:]
    o_ref, sf_ref, sb_ref, of_scr, state_scr, decay_scr = rest
    s = pl.program_id(2)
    c_len = CHUNK
    lseg = nchunk * c_len
    scale = HEAD_DIM ** -0.5

    @pl.when(s == 0)
    def _init():
        row = lax.broadcasted_iota(jnp.int32, (c_len, HEAD_DIM), 0).astype(F32)
        col = lax.broadcasted_iota(jnp.int32, (c_len, HEAD_DIM), 1).astype(F32)
        for d in range(2):
            lg = _log_sigmoid(dec_ref[d])
            diff = (row - col) if d == 0 else (col - row)
            decay_scr[d, 0] = jnp.where(diff >= 0, jnp.exp(lg * jnp.maximum(diff, 0.0)), 0.0)
            if d == 0:
                decay_scr[d, 1] = jnp.exp(lg * (row + 1.0))
                decay_scr[d, 2] = jnp.exp(lg * (c_len - 1.0 - row))
            else:
                decay_scr[d, 1] = jnp.exp(lg * (c_len - row))
                decay_scr[d, 2] = jnp.exp(lg * row)
            decay_scr[d, 3] = jnp.broadcast_to(jnp.exp(lg * float(c_len)), (c_len, HEAD_DIM))
        state_scr[...] = s0f_ref[...]

    @pl.when(s == nseg)
    def _init_b():
        state_scr[...] = s0b_ref[...]

    def chunk(c, d, state):
        rows = slice(c * c_len, (c + 1) * c_len)
        qf = q_ref[rows, :].astype(F32)
        kf = k_ref[rows, :].astype(F32)
        if rope:
            cos, sin = cos_ref[rows, :], sin_ref[rows, :]
            qf, kf = _rope(qf, cos, sin), _rope(kf, cos, sin)
        q = qf.astype(BF16)
        kf = kf * scale
        v = v_ref[rows, :].astype(BF16)
        sc = lax.dot_general(q, kf.astype(BF16), (((1,), (1,)), ((), ())), preferred_element_type=F32)
        sc = sc * decay_scr[d, 0]
        o = jnp.dot(sc.astype(BF16), v, preferred_element_type=F32)
        o = o + jnp.dot(q, state.astype(BF16), preferred_element_type=F32) * decay_scr[d, 1]
        kd = (kf * decay_scr[d, 2]).astype(BF16)
        state = state * decay_scr[d, 3] + lax.dot_general(kd, v, (((0,), (0,)), ((), ())), preferred_element_type=F32)
        return o, state

    @pl.when(s < nseg)
    def _fwd():
        state = state_scr[...]
        for c in range(nchunk):
            o, state = chunk(c, 0, state)
            of_scr[pl.ds(pl.multiple_of(s * lseg + c * c_len, c_len), c_len), :] = o
        state_scr[...] = state

        @pl.when(s == nseg - 1)
        def _():
            sf_ref[...] = state

    @pl.when(s >= nseg)
    def _bwd():
        seg = 2 * nseg - 1 - s
        state = state_scr[...]
        for c in reversed(range(nchunk)):
            o, state = chunk(c, 1, state)
            o = o + of_scr[pl.ds(pl.multiple_of(seg * lseg + c * c_len, c_len), c_len), :]
            o = o * lax.rsqrt(jnp.mean(o * o, axis=-1, keepdims=True) + EPS)
            g = g_ref[c * c_len:(c + 1) * c_len, :].astype(F32)
            o_ref[c * c_len:(c + 1) * c_len, :] = (g * _sigmoid(g) * o).astype(o_ref.dtype)
        state_scr[...] = state

        @pl.when(s == 2 * nseg - 1)
        def _():
            sb_ref[...] = state


def _retention(p, dec, layer, s0f, s0b, mix, rope, *, base_row, nseq, seq, heads, mix_width):
    t = p.shape[0]
    lseg = _pick(seq, (1024, 512, 256, 128))
    nseg = seq // lseg
    nchunk = lseg // CHUNK
    base_blk = base_row // lseg
    assert base_row % lseg == 0

    def seg_of(s):
        return jnp.where(s < nseg, s, 2 * nseg - 1 - s)

    def out_seg_of(s):
        return jnp.where(s < nseg, nseg - 1, 2 * nseg - 1 - s)

    def in_spec(col0):
        return pl.BlockSpec((lseg, HEAD_DIM), lambda b, h, s: (base_blk + b * nseg + seg_of(s), col0 + h))

    state_spec = pl.BlockSpec((None, None, HEAD_DIM, HEAD_DIM), lambda b, h, s: (b, h, 0, 0))
    in_specs = [
        pl.BlockSpec((None, 2, None, 1, HEAD_DIM), lambda b, h, s: (layer, 0, h, 0, 0)),
        in_spec(0), in_spec(heads), in_spec(2 * heads),
        pl.BlockSpec((lseg, HEAD_DIM), lambda b, h, s: (base_blk + b * nseg + out_seg_of(s), 3 * heads + h)),
        state_spec, state_spec,
    ]
    args = [dec, p, p, p, p, s0f, s0b]
    if rope is not None:
        in_specs += [pl.BlockSpec((lseg, HEAD_DIM), lambda b, h, s: (seg_of(s), 0))] * 2
        args += list(rope)
    aliases = {}
    if mix is not None:
        in_specs.append(pl.BlockSpec(memory_space=pl.ANY))
        args.append(mix)
        aliases = {len(args) - 1: 0}
    state_shape = jax.ShapeDtypeStruct((nseq, heads, HEAD_DIM, HEAD_DIM), F32)
    return pl.pallas_call(
        functools.partial(_retention_kernel, nseg=nseg, nchunk=nchunk, aliased=mix is not None,
                          rope=rope is not None),
        grid=(nseq, heads, 2 * nseg),
        in_specs=in_specs,
        out_specs=[
            pl.BlockSpec((lseg, HEAD_DIM), lambda b, h, s: (base_blk + b * nseg + out_seg_of(s), h)),
            state_spec, state_spec,
        ],
        out_shape=[jax.ShapeDtypeStruct((t, mix_width), BF16), state_shape, state_shape],
        scratch_shapes=[
            pltpu.VMEM((seq, HEAD_DIM), F32),
            pltpu.VMEM((HEAD_DIM, HEAD_DIM), F32),
            pltpu.VMEM((2, 4, CHUNK, HEAD_DIM), F32),
        ],
        input_output_aliases=aliases,
        compiler_params=_params(("arbitrary", "arbitrary", "arbitrary")),
        name="retention",
    )(*args)


def _cconv_kernel(am_ref, ap_ref, an_ref, bm_ref, bp_ref, bn_ref, w_ref, b_ref, lg_ref, lb_ref, mix_ref,
                  o_ref, u_scr, y_scr, us_scr, *, taps):
    del mix_ref
    j = pl.program_id(1)
    nj = pl.num_programs(1)
    tl, cw = o_ref.shape
    halo = CONV_HALO

    def glu(a_ref, b_ref_):
        return a_ref[...].astype(F32) * _sigmoid(b_ref_[...].astype(F32))

    u_scr[halo:halo + tl, :] = glu(am_ref, bm_ref)
    u_scr[0:halo, :] = jnp.where(j > 0, glu(ap_ref, bp_ref), 0.0)
    u_scr[halo + tl:halo + tl + halo, :] = jnp.where(j < nj - 1, glu(an_ref, bn_ref), 0.0)

    off0 = halo - taps // 2
    rb = 64 if tl % 64 == 0 else tl

    n_ext = us_scr.shape[1]

    def lane_body(c, carry):
        lanes = pl.ds(pl.multiple_of(c * 128, 128), 128)
        for ph in range(1, 8):
            us_scr[ph - 1] = u_scr[ph:ph + n_ext, lanes]
        for r in range(tl // rb):
            acc = jnp.zeros((rb, 128), F32)
            for k in range(taps):
                ph, a0 = (off0 + k) % 8, r * rb + 8 * ((off0 + k) // 8)
                src = u_scr[a0:a0 + rb, lanes] if ph == 0 else us_scr[ph - 1, a0:a0 + rb, :]
                acc = acc + w_ref[k:k + 1, lanes] * src
            y_scr[r * rb:(r + 1) * rb, lanes] = acc
        return carry

    lax.fori_loop(0, cw // 128, lane_body, 0)

    bias, ln_g, ln_b = b_ref[...], lg_ref[...], lb_ref[...]

    def row_body(r, carry):
        rows = pl.ds(pl.multiple_of(r * ROW_BLOCK, ROW_BLOCK), ROW_BLOCK)
        y = y_scr[rows, :] + bias
        mu = jnp.mean(y, axis=-1, keepdims=True)
        yc = y - mu
        var = jnp.mean(yc * yc, axis=-1, keepdims=True)
        z = yc * lax.rsqrt(var + EPS) * ln_g + ln_b
        o_ref[rows, :] = (z * _sigmoid(z)).astype(o_ref.dtype)
        return carry

    lax.fori_loop(0, tl // ROW_BLOCK, row_body, 0, unroll=LOOP_UNROLL)


def _conformer_conv(p, w, b, ln_g, ln_b, layer, mix, *, base_row, nseq, seq, col_a, col_out):
    t = p.shape[0]
    taps, cw = w.shape[1], w.shape[2]
    tl = _pick(seq, (512, 256, 128))
    nt = seq // tl
    assert base_row % tl == 0 and taps // 2 < CONV_HALO
    base_blk = base_row // tl
    hpt = tl // CONV_HALO
    last_halo = t // CONV_HALO - 1

    def main_spec(col):
        return pl.BlockSpec((tl, cw), lambda s, j: (base_blk + s * nt + j, col))

    def prev_spec(col):
        return pl.BlockSpec((CONV_HALO, cw), lambda s, j: (jnp.maximum((base_blk + s * nt + j) * hpt - 1, 0), col))

    def next_spec(col):
        return pl.BlockSpec((CONV_HALO, cw), lambda s, j: (jnp.minimum((base_blk + s * nt + j + 1) * hpt, last_halo), col))

    vec_spec = pl.BlockSpec((None, 1, cw), lambda s, j: (layer, 0, 0))
    return pl.pallas_call(
        functools.partial(_cconv_kernel, taps=taps),
        grid=(nseq, nt),
        in_specs=[
            main_spec(col_a), prev_spec(col_a), next_spec(col_a),
            main_spec(col_a + 1), prev_spec(col_a + 1), next_spec(col_a + 1),
            pl.BlockSpec((None, taps, cw), lambda s, j: (layer, 0, 0)),
            vec_spec, vec_spec, vec_spec,
            pl.BlockSpec(memory_space=pl.ANY),
        ],
        out_specs=pl.BlockSpec((tl, cw), lambda s, j: (base_blk + s * nt + j, col_out)),
        out_shape=jax.ShapeDtypeStruct(mix.shape, mix.dtype),
        scratch_shapes=[pltpu.VMEM((tl + 2 * CONV_HALO, cw), F32), pltpu.VMEM((tl, cw), F32),
                        pltpu.VMEM((7, tl + 8 * ((CONV_HALO + taps // 2) // 8), 128), F32)],
        input_output_aliases={10: 0},
        compiler_params=_params(("arbitrary", "arbitrary")),
        name="conformer_conv",
    )(p, p, p, p, p, p, w, b, ln_g, ln_b, mix)


def _residual_epilogue(x_ref, acc_scr, gate_ref, ng_ref, sh_ref, sc_ref, xo_ref, h_ref, o_ref, clear_acc=False):
    final = o_ref is not None
    gate, ng = gate_ref[...], ng_ref[...]
    if not final:
        sh, sc = sh_ref[...], sc_ref[...]

    def body(r, carry):
        rows = pl.ds(pl.multiple_of(r * ROW_BLOCK, ROW_BLOCK), ROW_BLOCK)
        xn = x_ref[rows, :] + gate * acc_scr[rows, :]
        if clear_acc:
            acc_scr[rows, :] = jnp.zeros((ROW_BLOCK, acc_scr.shape[1]), F32)
        if final:
            o_ref[rows, :] = xn * lax.rsqrt(jnp.mean(xn * xn, axis=-1, keepdims=True) + EPS) * ng
        else:
            xo_ref[rows, :] = xn
            h_ref[rows, :] = _norm_mod(xn, ng, sh, sc).astype(h_ref.dtype)
        return carry

    lax.fori_loop(0, x_ref.shape[0] // ROW_BLOCK, body, 0, unroll=LOOP_UNROLL)


def _mm_res_kernel(a_ref, w_ref, x_ref, gate_ref, ng_ref, *rest, final, h_aliased):
    if final:
        o_ref, acc_scr = rest
    else:
        sh_ref, sc_ref = rest[:2]
        xo_ref, h_ref, acc_scr = rest[3:] if h_aliased else rest[2:]
    k = pl.program_id(1)
    nk = pl.num_programs(1)
    prod = jnp.dot(a_ref[...], w_ref[...], preferred_element_type=F32)

    @pl.when(k == 0)
    def _():
        acc_scr[...] = prod

    @pl.when(k > 0)
    def _():
        acc_scr[...] += prod

    @pl.when(k == nk - 1)
    def _():
        if final:
            _residual_epilogue(x_ref, acc_scr, gate_ref, ng_ref, None, None, None, None, o_ref)
        else:
            _residual_epilogue(x_ref, acc_scr, gate_ref, ng_ref, sh_ref, sc_ref, xo_ref, h_ref, None)


def _mm_residual(a, w, layer, x_all, mods, gate_which, norm_g, norm_layer, next_mods, *, n_tiles, tm, tk, seq, batch,
                 final_rows=None, tile0=0, h_prev=None):
    t, d = x_all.shape
    kdim = a.shape[1]
    final = next_mods is None
    in_specs = [
        pl.BlockSpec((tm, tk), lambda i, k: (i + tile0, k)),
        pl.BlockSpec((None, tk, d), lambda i, k: (layer, k, 0)),
        pl.BlockSpec((tm, d), lambda i, k: (i + tile0, 0)),
        _mod_spec(layer, gate_which, tm, seq, batch, d, tile0),
    ]
    args = [a, w, x_all, mods]
    if final:
        in_specs.append(pl.BlockSpec((1, d), lambda i, k: (0, 0)))
        args.append(norm_g.reshape(1, d))
        out_specs = pl.BlockSpec((tm, d), lambda i, k: (i + tile0, 0))
        out_shape = jax.ShapeDtypeStruct((final_rows, d), F32)
        aliases = {}
    else:
        nl, shw, scw = next_mods
        in_specs += [
            pl.BlockSpec((None, 1, d), lambda i, k: (norm_layer, 0, 0)),
            _mod_spec(nl, shw, tm, seq, batch, d, tile0),
            _mod_spec(nl, scw, tm, seq, batch, d, tile0),
        ]
        args += [norm_g, mods, mods]
        row_spec = pl.BlockSpec((tm, d), lambda i, k: (i + tile0, 0))
        out_specs = [row_spec, row_spec]
        out_shape = [jax.ShapeDtypeStruct((t, d), F32), jax.ShapeDtypeStruct((t, d), BF16)]
        aliases = {2: 0}
        if h_prev is not None:
            in_specs.append(pl.BlockSpec(memory_space=pl.ANY))
            args.append(h_prev)
            aliases[len(args) - 1] = 1
    return pl.pallas_call(
        functools.partial(_mm_res_kernel, final=final, h_aliased=h_prev is not None),
        grid=(n_tiles, kdim // tk),
        in_specs=in_specs,
        out_specs=out_specs,
        out_shape=out_shape,
        scratch_shapes=[pltpu.VMEM((tm, d), F32)],
        input_output_aliases=aliases,
        compiler_params=_params(("arbitrary", "arbitrary")),
        name="matmul_residual",
    )(*args)


def _ffn_act_kernel(*refs, width, vertical, aliased):
    if vertical:
        gm_ref, gp_ref, gn_ref, val_ref, w_ref, b_ref = refs[:6]
        rest = refs[6:]
    else:
        gm_ref, val_ref, w_ref, b_ref = refs[:4]
        rest = refs[4:]
    if aliased:
        rest = rest[1:]
    o_ref, s_scr = rest
    j = pl.program_id(1)
    if not vertical:
        gp_ref = gn_ref = None

    _act_fill(gm_ref, gp_ref, gn_ref, s_scr, j > 0, j < pl.num_programs(1) - 1, width=width, vertical=vertical)
    for r in range(o_ref.shape[0] // ACT_ROWS):
        for c in range(o_ref.shape[1] // 128):
            o_ref[r * ACT_ROWS:(r + 1) * ACT_ROWS, c * 128:(c + 1) * 128] = _act_rows(
                r, c, val_ref, w_ref, b_ref, s_scr, width=width, vertical=vertical).astype(o_ref.dtype)


def _act_fill(gm_ref, gp_ref, gn_ref, s_scr, has_above, has_below, *, width, vertical):
    tl, tf = gm_ref.shape
    pad = 8
    base = pad + (width if vertical else 0)
    s_scr[0:pad, :] = jnp.zeros((pad, tf), F32)
    s_scr[s_scr.shape[0] - pad:, :] = jnp.zeros((pad, tf), F32)
    s_scr[base:base + tl, :] = gm_ref[...].astype(F32)
    if vertical:
        s_scr[pad:pad + width, :] = jnp.where(has_above, gp_ref[...].astype(F32), 0.0)
        s_scr[base + tl:base + tl + width, :] = jnp.where(has_below, gn_ref[...].astype(F32), 0.0)


def _act_rows(r, c, val_ref, w_ref, b_ref, s_scr, *, width, vertical):
    rb, tf = ACT_ROWS, 128
    lanes = slice(c * 128, (c + 1) * 128)
    base = 8 + (width if vertical else 0)
    dys = (-1, 0, 1) if vertical else (0,)
    col = (lax.broadcasted_iota(jnp.int32, (rb, tf), 0) + r * rb) % width
    acc = jnp.zeros((rb, tf), F32)
    for dx in (-1, 0, 1):
        part = jnp.zeros((rb, tf), F32)
        for dy in dys:
            tap = (dy + 1) * 3 + (dx + 1)
            start = base + r * rb + dy * width + dx
            part = part + w_ref[tap:tap + 1, lanes] * s_scr[start:start + rb, lanes]
        if dx == -1:
            part = jnp.where(col != 0, part, 0.0)
        elif dx == 1:
            part = jnp.where(col != width - 1, part, 0.0)
        acc = acc + part
    conv = acc + b_ref[:, lanes]
    val = val_ref[r * rb:(r + 1) * rb, lanes].astype(F32)
    return conv * _sigmoid(conv) * val


def _ffn_act(gv, w9, b, layer, act, *, base_row, nseq, seq, width, vertical):
    t = gv.shape[0]
    dff = w9.shape[2]
    tf = _pick(dff, (512, 256, 128))
    tl = _pick(seq, (512, 256, 128)) if vertical else seq
    assert tl % width == 0 and base_row % tl == 0 and tl % ACT_ROWS == 0
    nt = seq // tl
    nf = dff // tf
    base_blk = base_row // tl
    wpt = tl // width
    last_w = t // width - 1
    main = lambda s, j, f: (base_blk + s * nt + j, f)
    in_specs = [pl.BlockSpec((tl, tf), main)]
    args = [gv]
    if vertical:
        in_specs += [
            pl.BlockSpec((width, tf), lambda s, j, f: (jnp.maximum((base_blk + s * nt + j) * wpt - 1, 0), f)),
            pl.BlockSpec((width, tf), lambda s, j, f: (jnp.minimum((base_blk + s * nt + j + 1) * wpt, last_w), f)),
        ]
        args += [gv, gv]
    in_specs += [
        pl.BlockSpec((tl, tf), lambda s, j, f: (base_blk + s * nt + j, nf + f)),
        pl.BlockSpec((None, 9, tf), lambda s, j, f: (layer, 0, f)),
        pl.BlockSpec((None, 1, tf), lambda s, j, f: (layer, 0, f)),
    ]
    args += [gv, w9, b]
    aliases = {}
    if act is not None:
        in_specs.append(pl.BlockSpec(memory_space=pl.ANY))
        args.append(act)
        aliases = {len(args) - 1: 0}
    rows_scr = tl + 16 + (2 * width if vertical else 0)
    return pl.pallas_call(
        functools.partial(_ffn_act_kernel, width=width, vertical=vertical, aliased=act is not None),
        grid=(nseq, nt, nf),
        in_specs=in_specs,
        out_specs=pl.BlockSpec((tl, tf), main),
        out_shape=jax.ShapeDtypeStruct((t, dff), BF16),
        scratch_shapes=[pltpu.VMEM((rows_scr, tf), F32)],
        input_output_aliases=aliases,
        compiler_params=_params(("arbitrary", "arbitrary", "arbitrary")),
        name="ffn_act",
    )(*args)


def _ffn_down_kernel(gm_ref, gp_ref, gn_ref, val_ref, w9_ref, b9_ref, wd_ref, x_ref, gate_ref, ng_ref, *rest,
                     nk, n_steps, tiles_per_image, final):
    if final:
        o_ref, s_scr, act_a, act_b, acc_scr = rest
    else:
        sh_ref, sc_ref, xo_ref, h_ref, s_scr, act_a, act_b, acc_scr = rest
    s = pl.program_id(0)
    j = (jnp.minimum(s, n_steps - 1) // nk) % tiles_per_image
    consumed_chunk = jnp.maximum(s - 1, 0) % nk
    slots = (act_a, act_b)
    n_rows = gm_ref.shape[0] // ACT_ROWS
    n_lane = gm_ref.shape[1] // 128
    n_cols = acc_scr.shape[1] // MXU_COLS
    tk = gm_ref.shape[1]
    k_parts = [slice(k0, min(k0 + MXU_K, tk)) for k0 in range(0, tk, MXU_K)]

    def step(produce_slot, consume_slot):
        def consume(n):
            ks, cols = k_parts[n // n_cols], slice((n % n_cols) * MXU_COLS, (n % n_cols + 1) * MXU_COLS)
            acc_scr[:, cols] += jnp.dot(slots[consume_slot][:, ks], wd_ref[ks, cols], preferred_element_type=F32)

        def produce(r, c):
            slots[produce_slot][r * ACT_ROWS:(r + 1) * ACT_ROWS, c * 128:(c + 1) * 128] = _act_rows(
                r, c, val_ref, w9_ref, b9_ref, s_scr, width=GRID_W, vertical=True).astype(BF16)

        pieces = [(r, c) for r in range(n_rows) for c in range(n_lane)] if produce_slot is not None else []
        n_consume = n_cols * len(k_parts) if consume_slot is not None else 0
        done = 0
        if n_consume:
            consume(0)
            done = 1
        if produce_slot is not None:
            _act_fill(gm_ref, gp_ref, gn_ref, s_scr, j > 0, j < tiles_per_image - 1, width=GRID_W, vertical=True)
        for idx, (r, c) in enumerate(pieces):
            produce(r, c)
            target = 1 + (idx + 1) * (n_consume - 1) // len(pieces) if n_consume else 0
            while done < target:
                consume(done)
                done += 1
        while done < n_consume:
            consume(done)
            done += 1

    @pl.when(s == 0)
    def _():
        acc_scr[...] = jnp.zeros(acc_scr.shape, F32)
        step(0, None)

    middle = jnp.logical_and(s > 0, s < n_steps)

    @pl.when(jnp.logical_and(middle, s % 2 == 0))
    def _():
        step(0, 1)

    @pl.when(jnp.logical_and(middle, s % 2 == 1))
    def _():
        step(1, 0)

    @pl.when(s == n_steps)
    def _():
        step(None, (n_steps - 1) % 2)

    @pl.when(jnp.logical_and(s > 0, consumed_chunk == nk - 1))
    def _():
        if final:
            _residual_epilogue(x_ref, acc_scr, gate_ref, ng_ref, None, None, None, None, o_ref, clear_acc=True)
        else:
            _residual_epilogue(x_ref, acc_scr, gate_ref, ng_ref, sh_ref, sc_ref, xo_ref, h_ref, None, clear_acc=True)


def _ffn_down_fused(gv, w9, b9, w_down, layer, x_all, mods, gate_which, norm_g, norm_layer, next_mods, *, n_lat, seq,
                    batch):
    t, d = x_all.shape
    dff = w9.shape[2]
    tk = _pick(dff, (1408, 512, 256, 128))
    tl = _pick(seq, (512, 256, 128))
    nk = dff // tk
    n_steps = (n_lat // tl) * nk
    assert tl % GRID_W == 0 and tk % 128 == 0
    wpt = tl // GRID_W
    last_w = t // GRID_W - 1
    final = next_mods is None

    def produced(s):
        sp = jnp.minimum(s, n_steps - 1)
        return sp // nk, sp % nk

    def consumed(s):
        sc = jnp.maximum(s - 1, 0)
        return sc // nk, sc % nk

    consumed_tile = lambda s: consumed(s)[0]
    in_specs = [
        pl.BlockSpec((tl, tk), lambda s: produced(s)),
        pl.BlockSpec((GRID_W, tk), lambda s: (jnp.maximum(produced(s)[0] * wpt - 1, 0), produced(s)[1])),
        pl.BlockSpec((GRID_W, tk), lambda s: (jnp.minimum((produced(s)[0] + 1) * wpt, last_w), produced(s)[1])),
        pl.BlockSpec((tl, tk), lambda s: (produced(s)[0], nk + produced(s)[1])),
        pl.BlockSpec((None, 9, tk), lambda s: (layer, 0, produced(s)[1])),
        pl.BlockSpec((None, 1, tk), lambda s: (layer, 0, produced(s)[1])),
        pl.BlockSpec((None, tk, d), lambda s: (layer, consumed(s)[1], 0)),
        pl.BlockSpec((tl, d), lambda s: (consumed_tile(s), 0)),
        _mod_spec(layer, gate_which, tl, seq, batch, d, tile_of=consumed_tile),
    ]
    args = [gv, gv, gv, gv, w9, b9, w_down, x_all, mods]
    row_spec = pl.BlockSpec((tl, d), lambda s: (consumed_tile(s), 0))
    if final:
        in_specs.append(pl.BlockSpec((1, d), lambda s: (0, 0)))
        args.append(norm_g.reshape(1, d))
        out_specs, out_shape, aliases = row_spec, jax.ShapeDtypeStruct((n_lat, d), F32), {}
    else:
        nl, shw, scw = next_mods
        in_specs += [
            pl.BlockSpec((None, 1, d), lambda s: (norm_layer, 0, 0)),
            _mod_spec(nl, shw, tl, seq, batch, d, tile_of=consumed_tile),
            _mod_spec(nl, scw, tl, seq, batch, d, tile_of=consumed_tile),
        ]
        args += [norm_g, mods, mods]
        out_specs = [row_spec, row_spec]
        out_shape = [jax.ShapeDtypeStruct((t, d), F32), jax.ShapeDtypeStruct((t, d), BF16)]
        aliases = {7: 0}
    return pl.pallas_call(
        functools.partial(_ffn_down_kernel, nk=nk, n_steps=n_steps, tiles_per_image=seq // tl, final=final),
        grid=(n_steps + 1,),
        in_specs=in_specs,
        out_specs=out_specs,
        out_shape=out_shape,
        scratch_shapes=[
            pltpu.VMEM((tl + 2 * GRID_W + 16, tk), F32),
            pltpu.VMEM((tl, tk), BF16),
            pltpu.VMEM((tl, tk), BF16),
            pltpu.VMEM((tl, d), F32),
        ],
        input_output_aliases=aliases,
        compiler_params=_params(("arbitrary",)),
        name="ffn_down_fused",
    )(*args)


def _rope_tables(seq):
    t = jnp.arange(seq)
    quarter = HEAD_DIM // 4
    inv_freq = 1.0 / (ROPE_THETA ** (jnp.arange(0, quarter, dtype=F32) / quarter))
    ang_r = (t // GRID_W).astype(F32)[:, None] * inv_freq[None, :]
    ang_c = (t % GRID_W).astype(F32)[:, None] * inv_freq[None, :]
    cos = jnp.concatenate([jnp.cos(ang_r)] * 2 + [jnp.cos(ang_c)] * 2, axis=-1)
    sin = jnp.concatenate([-jnp.sin(ang_r), jnp.sin(ang_r), -jnp.sin(ang_c), jnp.sin(ang_c)], axis=-1)
    return cos, sin


def kernel(x, c, ctx, c_ctx, w_mod, b_mod, norm1_g, norm2_g, w_in, ret_decay_f, ret_decay_b, conv_dw_w, conv_dw_b,
           conv_ln_g, conv_ln_b, w_out, ffn_w_up, ffn_dw_w, ffn_dw_b, ffn_w_down, final_norm_g):
    bsz, seq, d = x.shape
    lc = ctx.shape[1]
    depth = w_mod.shape[0]
    heads = ret_decay_f.shape[1]
    rw = heads * HEAD_DIM
    cw = conv_dw_w.shape[2]
    dff = ffn_dw_b.shape[1]
    n_lat, n_ctx = bsz * seq, bsz * lc
    t = n_lat + n_ctx
    assert bsz + 1 <= MOD_ROWS and rw % cw == 0 and seq % GRID_W == 0 and w_mod.shape[2] == N_MOD * d

    tm = _pick(_gcd(seq, n_ctx), (1024, 512, 256, 128))
    tm_res = min(tm, 512)
    tn_in = _pick(w_in.shape[2], (1024, 512, 256, 128))
    tn_up = _pick(2 * dff, (1024, 512, 256, 128))
    tk_down = _pick(dff, (1408, 1024, 512, 256, 128))

    cs = jnp.zeros((MOD_ROWS, d), F32).at[:bsz].set(c).at[bsz].set(c_ctx)
    mods = _modulation(cs, w_mod, b_mod).reshape(depth * MOD_ROWS * N_MOD, 1, d)

    x_all = jnp.concatenate([x.reshape(n_lat, d), ctx.reshape(n_ctx, d)], axis=0)
    w_in_b, w_out_b = w_in.astype(BF16), w_out.astype(BF16)
    w_up_b, w_down_b = ffn_w_up.astype(BF16), ffn_w_down.astype(BF16)
    cos, sin = _rope_tables(seq)
    dec = jnp.broadcast_to(jnp.stack([ret_decay_f, ret_decay_b], axis=1).astype(F32)[:, :, :, None, None],
                           (depth, 2, heads, 1, HEAD_DIM))
    n1 = norm1_g.reshape(depth, 1, d)
    n2 = norm2_g.reshape(depth, 1, d)
    cb = conv_dw_b.reshape(depth, 1, cw)
    clg = conv_ln_g.reshape(depth, 1, cw)
    clb = conv_ln_b.reshape(depth, 1, cw)
    w9 = ffn_dw_w.reshape(depth, 9, dff)
    fb = ffn_dw_b.reshape(depth, 1, dff)
    zero_state = jnp.zeros((bsz, heads, HEAD_DIM, HEAD_DIM), F32)

    tiles_all, tiles_lat = t // tm, n_lat // tm
    res_all, res_lat = t // tm_res, n_lat // tm_res
    mix_w = rw + cw
    col_a = 4 * rw // cw

    h = _first_norm(x_all, n1, mods, 0, tm, seq, bsz)
    out = None
    for l in range(depth):
        last = l == depth - 1
        p = _projection(h, w_in_b, l, tiles_all, tm, tn_in)
        mix, s_f, s_b = _retention(p, dec, l, zero_state, zero_state, None, None, base_row=n_lat, nseq=bsz, seq=lc,
                                   heads=heads, mix_width=mix_w)
        mix, _, _ = _retention(p, dec, l, s_f, s_b, mix, (cos, sin), base_row=0, nseq=bsz, seq=seq, heads=heads,
                               mix_width=mix_w)
        mix = _conformer_conv(p, conv_dw_w, cb, clg, clb, l, mix, base_row=0, nseq=bsz, seq=seq, col_a=col_a,
                              col_out=rw // cw)
        if not last:
            mix = _conformer_conv(p, conv_dw_w, cb, clg, clb, l, mix, base_row=n_lat, nseq=bsz, seq=lc, col_a=col_a,
                                  col_out=rw // cw)
        x_all, h2 = _mm_residual(mix, w_out_b, l, x_all, mods, 2, n2, l, (l, 3, 4),
                                 n_tiles=res_lat if last else res_all, tm=tm_res, tk=mix_w, seq=seq, batch=bsz)
        gv = _projection(h2, w_up_b, l, tiles_lat if last else tiles_all, tm, tn_up)
        if not last:
            act_ctx = _ffn_act(gv, w9, fb, l, None, base_row=n_lat, nseq=bsz, seq=lc, width=lc, vertical=False)
            x_all, h = _ffn_down_fused(gv, w9, fb, w_down_b, l, x_all, mods, 5, n1, l + 1, (l + 1, 0, 1),
                                       n_lat=n_lat, seq=seq, batch=bsz)
            x_all, h = _mm_residual(act_ctx, w_down_b, l, x_all, mods, 5, n1, l + 1, (l + 1, 0, 1),
                                    n_tiles=res_all - res_lat, tm=tm_res, tk=tk_down, seq=seq, batch=bsz,
                                    tile0=res_lat, h_prev=h)
        else:
            out = _ffn_down_fused(gv, w9, fb, w_down_b, l, x_all, mods, 5, final_norm_g, 0, None,
                                  n_lat=n_lat, seq=seq, batch=bsz)
    return out.reshape(bsz, seq, d)


def _gcd(a, b):
    while b:
        a, b = b, a % b
    return a
```

```python
import functools

import jax
import jax.numpy as jnp
from jax import lax
from jax.experimental import pallas as pl
from jax.experimental.pallas import tpu as pltpu

F32 = jnp.float32
BF16 = jnp.bfloat16

GRID_W = 64
HEAD_DIM = 128
CHUNK = 128
N_MOD = 6
ROPE_THETA = 10000.0
EPS = 1e-6
MOD_ROWS = 8
CONV_HALO = 16
ACT_ROWS = 64
MXU_COLS = 256
MXU_K = 512
ROW_BLOCK = 16
LOOP_UNROLL = 4
VMEM_LIMIT = 52 * 1024 * 1024

ACT_DTYPE = BF16


def _pick(n, candidates):
    for c in candidates:
        if n % c == 0:
            return c
    raise ValueError(f"no tile for {n} in {candidates}")


def _params(sem):
    return pltpu.CompilerParams(dimension_semantics=sem, vmem_limit_bytes=VMEM_LIMIT)


def _sigmoid(x):
    return jax.nn.sigmoid(x)


def _norm_mod(x, g, sh, sc):
    xn = x * lax.rsqrt(jnp.mean(x * x, axis=-1, keepdims=True) + EPS)
    return (xn * g) * (1.0 + sc) + sh


def _mod_kernel(c_ref, w_ref, b_ref, o_ref):
    c = c_ref[...]
    s = c * _sigmoid(c)
    o_ref[...] = jnp.dot(s.astype(BF16), w_ref[...].astype(BF16), preferred_element_type=F32) + b_ref[...]


def _modulation(cs, w_mod, b_mod):
    depth, d, n = w_mod.shape
    tn = _pick(n, (1024, 512, 256, 128))
    return pl.pallas_call(
        _mod_kernel,
        grid=(depth, n // tn),
        in_specs=[
            pl.BlockSpec((MOD_ROWS, d), lambda l, j: (0, 0)),
            pl.BlockSpec((None, d, tn), lambda l, j: (l, 0, j)),
            pl.BlockSpec((None, 1, tn), lambda l, j: (l, 0, j)),
        ],
        out_specs=pl.BlockSpec((None, MOD_ROWS, tn), lambda l, j: (l, 0, j)),
        out_shape=jax.ShapeDtypeStruct((depth, MOD_ROWS, n), F32),
        compiler_params=_params(("arbitrary", "arbitrary")),
        name="modulation",
    )(cs, w_mod, b_mod.reshape(depth, 1, n))


def _mod_spec(layer, which, tm, seq, batch, d, tile0=0, tile_of=None):
    def imap(i, *rest):
        tile = (i + tile0) if tile_of is None else tile_of(i, *rest)
        grp = jnp.minimum((tile * tm) // seq, batch)
        return ((layer * MOD_ROWS + grp) * N_MOD + which, 0, 0)
    return pl.BlockSpec((None, 1, d), imap)


def _first_norm_kernel(xl_ref, xc_ref, g_ref, sh_ref, sc_ref, x_ref, h_ref, *, n_lat_tiles):
    g, sh, sc = g_ref[...], sh_ref[...], sc_ref[...]

    def run(src_ref):
        def body(r, carry):
            rows = pl.ds(pl.multiple_of(r * ROW_BLOCK, ROW_BLOCK), ROW_BLOCK)
            xr = src_ref[rows, :]
            x_ref[rows, :] = xr
            h_ref[rows, :] = _norm_mod(xr, g, sh, sc).astype(h_ref.dtype)
            return carry

        lax.fori_loop(0, x_ref.shape[0] // ROW_BLOCK, body, 0, unroll=LOOP_UNROLL)

    @pl.when(pl.program_id(0) < n_lat_tiles)
    def _():
        run(xl_ref)

    @pl.when(pl.program_id(0) >= n_lat_tiles)
    def _():
        run(xc_ref)


def _first_norm(x_lat, x_ctx, norm_g, mods, layer, tm, seq, batch):
    n_lat, d = x_lat.shape
    n_ctx = x_ctx.shape[0]
    t = n_lat + n_ctx
    n_lat_tiles, n_ctx_tiles = n_lat // tm, n_ctx // tm
    return pl.pallas_call(
        functools.partial(_first_norm_kernel, n_lat_tiles=n_lat_tiles),
        grid=(t // tm,),
        in_specs=[
            pl.BlockSpec((tm, d), lambda i: (jnp.minimum(i, n_lat_tiles - 1), 0)),
            pl.BlockSpec((tm, d), lambda i: (jnp.clip(i - n_lat_tiles, 0, n_ctx_tiles - 1), 0)),
            pl.BlockSpec((None, 1, d), lambda i: (layer, 0, 0)),
            _mod_spec(layer, 0, tm, seq, batch, d),
            _mod_spec(layer, 1, tm, seq, batch, d),
        ],
        out_specs=[pl.BlockSpec((tm, d), lambda i: (i, 0)), pl.BlockSpec((tm, d), lambda i: (i, 0))],
        out_shape=[jax.ShapeDtypeStruct((t, d), F32), jax.ShapeDtypeStruct((t, d), BF16)],
        compiler_params=_params(("arbitrary",)),
        name="first_norm",
    )(x_lat, x_ctx, norm_g, mods, mods)


def _proj_kernel(a_ref, w_ref, o_ref):
    o_ref[...] = jnp.dot(a_ref[...], w_ref[...], preferred_element_type=F32).astype(o_ref.dtype)


def _projection(a, w, layer, n_tiles, tm, tn):
    t, kdim = a.shape
    n = w.shape[2]
    return pl.pallas_call(
        _proj_kernel,
        grid=(n_tiles, n // tn),
        in_specs=[
            pl.BlockSpec((tm, kdim), lambda i, j: (i, 0)),
            pl.BlockSpec((None, kdim, tn), lambda i, j: (layer, 0, j)),
        ],
        out_specs=pl.BlockSpec((tm, tn), lambda i, j: (i, j)),
        out_shape=jax.ShapeDtypeStruct((t, n), ACT_DTYPE),
        compiler_params=_params(("arbitrary", "arbitrary")),
        name="projection",
    )(a, w)


def _log_sigmoid(x):
    return jnp.minimum(x, 0.0) - jnp.log1p(jnp.exp(-jnp.abs(x)))


def _rope(x, cos, sin_signed):
    lane = lax.broadcasted_iota(jnp.int32, x.shape, 1)
    first = (lane % (HEAD_DIM // 2)) < (HEAD_DIM // 4)
    partner = jnp.where(first, pltpu.roll(x, HEAD_DIM - HEAD_DIM // 4, 1), pltpu.roll(x, HEAD_DIM // 4, 1))
    return x * cos + partner * sin_signed


def _retention_kernel(dec_ref, q_ref, k_ref, v_ref, g_ref, s0f_ref, s0b_ref, *rest, nseg, nchunk, aliased, rope):
    if rope:
        cos_ref, sin_ref = rest[:2]
        rest = rest[2:]
    if aliased:
        rest = rest[1:]
    o_ref, sf_ref, sb_ref, of_scr, state_scr, decay_scr = rest
    s = pl.program_id(2)
    c_len = CHUNK
    lseg = nchunk * c_len
    scale = HEAD_DIM ** -0.5

    @pl.when(s == 0)
    def _init():
        row = lax.broadcasted_iota(jnp.int32, (c_len, HEAD_DIM), 0).astype(F32)
        col = lax.broadcasted_iota(jnp.int32, (c_len, HEAD_DIM), 1).astype(F32)
        for d in range(2):
            lg = _log_sigmoid(dec_ref[d])
            diff = (row - col) if d == 0 else (col - row)
            decay_scr[d, 0] = jnp.where(diff >= 0, jnp.exp(lg * jnp.maximum(diff, 0.0)), 0.0)
            if d == 0:
                decay_scr[d, 1] = jnp.exp(lg * (row + 1.0))
                decay_scr[d, 2] = jnp.exp(lg * (c_len - 1.0 - row))
            else:
                decay_scr[d, 1] = jnp.exp(lg * (c_len - row))
                decay_scr[d, 2] = jnp.exp(lg * row)
            decay_scr[d, 3] = jnp.broadcast_to(jnp.exp(lg * float(c_len)), (c_len, HEAD_DIM))
        state_scr[...] = s0f_ref[...]

    @pl.when(s == nseg)
    def _init_b():
        state_scr[...] = s0b_ref[...]

    def chunk(c, d, state):
        rows = slice(c * c_len, (c + 1) * c_len)
        qf = q_ref[rows, :].astype(F32)
        kf = k_ref[rows, :].astype(F32)
        if rope:
            cos, sin = cos_ref[rows, :], sin_ref[rows, :]
            qf, kf = _rope(qf, cos, sin), _rope(kf, cos, sin)
        q = qf.astype(BF16)
        kf = kf * scale
        v = v_ref[rows, :].astype(BF16)
        sc = lax.dot_general(q, kf.astype(BF16), (((1,), (1,)), ((), ())), preferred_element_type=F32)
        sc = sc * decay_scr[d, 0]
        o = jnp.dot(sc.astype(BF16), v, preferred_element_type=F32)
        o = o + jnp.dot(q, state.astype(BF16), preferred_element_type=F32) * decay_scr[d, 1]
        kd = (kf * decay_scr[d, 2]).astype(BF16)
        state = state * decay_scr[d, 3] + lax.dot_general(kd, v, (((0,), (0,)), ((), ())), preferred_element_type=F32)
        return o, state

    @pl.when(s < nseg)
    def _fwd():
        state = state_scr[...]
        for c in range(nchunk):
            o, state = chunk(c, 0, state)
            of_scr[pl.ds(pl.multiple_of(s * lseg + c * c_len, c_len), c_len), :] = o
        state_scr[...] = state

        @pl.when(s == nseg - 1)
        def _():
            sf_ref[...] = state

    @pl.when(s >= nseg)
    def _bwd():
        seg = 2 * nseg - 1 - s
        state = state_scr[...]
        for c in reversed(range(nchunk)):
            o, state = chunk(c, 1, state)
            o = o + of_scr[pl.ds(pl.multiple_of(seg * lseg + c * c_len, c_len), c_len), :]
            o = o * lax.rsqrt(jnp.mean(o * o, axis=-1, keepdims=True) + EPS)
            g = g_ref[c * c_len:(c + 1) * c_len, :].astype(F32)
            o_ref[c * c_len:(c + 1) * c_len, :] = (g * _sigmoid(g) * o).astype(o_ref.dtype)
        state_scr[...] = state

        @pl.when(s == 2 * nseg - 1)
        def _():
            sb_ref[...] = state


def _retention(p, dec, layer, s0f, s0b, mix, rope, *, base_row, nseq, seq, heads, mix_width):
    t = p.shape[0]
    lseg = _pick(seq, (2048, 1024, 512, 256, 128))
    nseg = seq // lseg
    nchunk = lseg // CHUNK
    base_blk = base_row // lseg
    assert base_row % lseg == 0

    def seg_of(s):
        return jnp.where(s < nseg, s, 2 * nseg - 1 - s)

    def out_seg_of(s):
        return jnp.where(s < nseg, nseg - 1, 2 * nseg - 1 - s)

    def in_spec(col0):
        return pl.BlockSpec((lseg, HEAD_DIM), lambda b, h, s: (base_blk + b * nseg + seg_of(s), col0 + h))

    state_spec = pl.BlockSpec((None, None, HEAD_DIM, HEAD_DIM), lambda b, h, s: (b, h, 0, 0))
    in_specs = [
        pl.BlockSpec((None, 2, None, 1, HEAD_DIM), lambda b, h, s: (layer, 0, h, 0, 0)),
        in_spec(0), in_spec(heads), in_spec(2 * heads),
        pl.BlockSpec((lseg, HEAD_DIM), lambda b, h, s: (base_blk + b * nseg + out_seg_of(s), 3 * heads + h)),
        state_spec, state_spec,
    ]
    args = [dec, p, p, p, p, s0f, s0b]
    if rope is not None:
        in_specs += [pl.BlockSpec((lseg, HEAD_DIM), lambda b, h, s: (seg_of(s), 0))] * 2
        args += list(rope)
    aliases = {}
    if mix is not None:
        in_specs.append(pl.BlockSpec(memory_space=pl.ANY))
        args.append(mix)
        aliases = {len(args) - 1: 0}
    state_shape = jax.ShapeDtypeStruct((nseq, heads, HEAD_DIM, HEAD_DIM), F32)
    return pl.pallas_call(
        functools.partial(_retention_kernel, nseg=nseg, nchunk=nchunk, aliased=mix is not None,
                          rope=rope is not None),
        grid=(nseq, heads, 2 * nseg),
        in_specs=in_specs,
        out_specs=[
            pl.BlockSpec((lseg, HEAD_DIM), lambda b, h, s: (base_blk + b * nseg + out_seg_of(s), h)),
            state_spec, state_spec,
        ],
        out_shape=[jax.ShapeDtypeStruct((t, mix_width), BF16), state_shape, state_shape],
        scratch_shapes=[
            pltpu.VMEM((seq, HEAD_DIM), F32),
            pltpu.VMEM((HEAD_DIM, HEAD_DIM), F32),
            pltpu.VMEM((2, 4, CHUNK, HEAD_DIM), F32),
        ],
        input_output_aliases=aliases,
        compiler_params=_params(("arbitrary", "arbitrary", "arbitrary")),
        name="retention",
    )(*args)


def _cconv_kernel(am_ref, ap_ref, an_ref, bm_ref, bp_ref, bn_ref, w_ref, b_ref, lg_ref, lb_ref, mix_ref,
                  o_ref, u_scr, y_scr, us_scr, *, taps):
    del mix_ref
    j = pl.program_id(1)
    nj = pl.num_programs(1)
    tl, cw = o_ref.shape
    halo = CONV_HALO

    def glu(a_ref, b_ref_):
        return a_ref[...].astype(F32) * _sigmoid(b_ref_[...].astype(F32))

    u_scr[halo:halo + tl, :] = glu(am_ref, bm_ref)
    u_scr[0:halo, :] = jnp.where(j > 0, glu(ap_ref, bp_ref), 0.0)
    u_scr[halo + tl:halo + tl + halo, :] = jnp.where(j < nj - 1, glu(an_ref, bn_ref), 0.0)

    off0 = halo - taps // 2
    rb = 64 if tl % 64 == 0 else tl

    n_ext = us_scr.shape[1]

    def lane_body(c, carry):
        lanes = pl.ds(pl.multiple_of(c * 128, 128), 128)
        for ph in range(1, 8):
            us_scr[ph - 1] = u_scr[ph:ph + n_ext, lanes]
        for r in range(tl // rb):
            acc = jnp.zeros((rb, 128), F32)
            for k in range(taps):
                ph, a0 = (off0 + k) % 8, r * rb + 8 * ((off0 + k) // 8)
                src = u_scr[a0:a0 + rb, lanes] if ph == 0 else us_scr[ph - 1, a0:a0 + rb, :]
                acc = acc + w_ref[k:k + 1, lanes] * src
            y_scr[r * rb:(r + 1) * rb, lanes] = acc
        return carry

    lax.fori_loop(0, cw // 128, lane_body, 0)

    bias, ln_g, ln_b = b_ref[...], lg_ref[...], lb_ref[...]

    def row_body(r, carry):
        rows = pl.ds(pl.multiple_of(r * ROW_BLOCK, ROW_BLOCK), ROW_BLOCK)
        y = y_scr[rows, :] + bias
        mu = jnp.mean(y, axis=-1, keepdims=True)
        yc = y - mu
        var = jnp.mean(yc * yc, axis=-1, keepdims=True)
        z = yc * lax.rsqrt(var + EPS) * ln_g + ln_b
        o_ref[rows, :] = (z * _sigmoid(z)).astype(o_ref.dtype)
        return carry

    lax.fori_loop(0, tl // ROW_BLOCK, row_body, 0, unroll=LOOP_UNROLL)


def _conformer_conv(p, w, b, ln_g, ln_b, layer, mix, *, base_row, nseq, seq, col_a, col_out):
    t = p.shape[0]
    taps, cw = w.shape[1], w.shape[2]
    tl = _pick(seq, (512, 256, 128))
    nt = seq // tl
    assert base_row % tl == 0 and taps // 2 < CONV_HALO
    base_blk = base_row // tl
    hpt = tl // CONV_HALO
    last_halo = t // CONV_HALO - 1

    def main_spec(col):
        return pl.BlockSpec((tl, cw), lambda s, j: (base_blk + s * nt + j, col))

    def prev_spec(col):
        return pl.BlockSpec((CONV_HALO, cw), lambda s, j: (jnp.maximum((base_blk + s * nt + j) * hpt - 1, 0), col))

    def next_spec(col):
        return pl.BlockSpec((CONV_HALO, cw), lambda s, j: (jnp.minimum((base_blk + s * nt + j + 1) * hpt, last_halo), col))

    vec_spec = pl.BlockSpec((None, 1, cw), lambda s, j: (layer, 0, 0))
    return pl.pallas_call(
        functools.partial(_cconv_kernel, taps=taps),
        grid=(nseq, nt),
        in_specs=[
            main_spec(col_a), prev_spec(col_a), next_spec(col_a),
            main_spec(col_a + 1), prev_spec(col_a + 1), next_spec(col_a + 1),
            pl.BlockSpec((None, taps, cw), lambda s, j: (layer, 0, 0)),
            vec_spec, vec_spec, vec_spec,
            pl.BlockSpec(memory_space=pl.ANY),
        ],
        out_specs=pl.BlockSpec((tl, cw), lambda s, j: (base_blk + s * nt + j, col_out)),
        out_shape=jax.ShapeDtypeStruct(mix.shape, mix.dtype),
        scratch_shapes=[pltpu.VMEM((tl + 2 * CONV_HALO, cw), F32), pltpu.VMEM((tl, cw), F32),
                        pltpu.VMEM((7, tl + 8 * ((CONV_HALO + taps // 2) // 8), 128), F32)],
        input_output_aliases={10: 0},
        compiler_params=_params(("arbitrary", "arbitrary")),
        name="conformer_conv",
    )(p, p, p, p, p, p, w, b, ln_g, ln_b, mix)


def _residual_epilogue(x_ref, acc_scr, gate_ref, ng_ref, sh_ref, sc_ref, xo_ref, h_ref, o_ref, clear_acc=False):
    final = o_ref is not None
    gate, ng = gate_ref[...], ng_ref[...]
    if not final:
        sh, sc = sh_ref[...], sc_ref[...]

    def body(r, carry):
        rows = pl.ds(pl.multiple_of(r * ROW_BLOCK, ROW_BLOCK), ROW_BLOCK)
        xn = x_ref[rows, :] + gate * acc_scr[rows, :]
        if clear_acc:
            acc_scr[rows, :] = jnp.zeros((ROW_BLOCK, acc_scr.shape[1]), F32)
        if final:
            o_ref[rows, :] = xn * lax.rsqrt(jnp.mean(xn * xn, axis=-1, keepdims=True) + EPS) * ng
        else:
            xo_ref[rows, :] = xn
            h_ref[rows, :] = _norm_mod(xn, ng, sh, sc).astype(h_ref.dtype)
        return carry

    lax.fori_loop(0, x_ref.shape[0] // ROW_BLOCK, body, 0, unroll=LOOP_UNROLL)


def _mm_res_kernel(a_ref, w_ref, x_ref, gate_ref, ng_ref, *rest, final, h_aliased):
    if final:
        o_ref, acc_scr = rest
    else:
        sh_ref, sc_ref = rest[:2]
        xo_ref, h_ref, acc_scr = rest[3:] if h_aliased else rest[2:]
    k = pl.program_id(1)
    nk = pl.num_programs(1)
    prod = jnp.dot(a_ref[...], w_ref[...], preferred_element_type=F32)

    @pl.when(k == 0)
    def _():
        acc_scr[...] = prod

    @pl.when(k > 0)
    def _():
        acc_scr[...] += prod

    @pl.when(k == nk - 1)
    def _():
        if final:
            _residual_epilogue(x_ref, acc_scr, gate_ref, ng_ref, None, None, None, None, o_ref)
        else:
            _residual_epilogue(x_ref, acc_scr, gate_ref, ng_ref, sh_ref, sc_ref, xo_ref, h_ref, None)


def _mm_residual(a, w, layer, x_all, mods, gate_which, norm_g, norm_layer, next_mods, *, n_tiles, tm, tk, seq, batch,
                 final_rows=None, tile0=0, h_prev=None):
    t, d = x_all.shape
    kdim = a.shape[1]
    final = next_mods is None
    in_specs = [
        pl.BlockSpec((tm, tk), lambda i, k: (i + tile0, k)),
        pl.BlockSpec((None, tk, d), lambda i, k: (layer, k, 0)),
        pl.BlockSpec((tm, d), lambda i, k: (i + tile0, 0)),
        _mod_spec(layer, gate_which, tm, seq, batch, d, tile0),
    ]
    args = [a, w, x_all, mods]
    if final:
        in_specs.append(pl.BlockSpec((1, d), lambda i, k: (0, 0)))
        args.append(norm_g.reshape(1, d))
        out_specs = pl.BlockSpec((tm, d), lambda i, k: (i + tile0, 0))
        out_shape = jax.ShapeDtypeStruct((final_rows, d), F32)
        aliases = {}
    else:
        nl, shw, scw = next_mods
        in_specs += [
            pl.BlockSpec((None, 1, d), lambda i, k: (norm_layer, 0, 0)),
            _mod_spec(nl, shw, tm, seq, batch, d, tile0),
            _mod_spec(nl, scw, tm, seq, batch, d, tile0),
        ]
        args += [norm_g, mods, mods]
        row_spec = pl.BlockSpec((tm, d), lambda i, k: (i + tile0, 0))
        out_specs = [row_spec, row_spec]
        out_shape = [jax.ShapeDtypeStruct((t, d), F32), jax.ShapeDtypeStruct((t, d), BF16)]
        aliases = {2: 0}
        if h_prev is not None:
            in_specs.append(pl.BlockSpec(memory_space=pl.ANY))
            args.append(h_prev)
            aliases[len(args) - 1] = 1
    return pl.pallas_call(
        functools.partial(_mm_res_kernel, final=final, h_aliased=h_prev is not None),
        grid=(n_tiles, kdim // tk),
        in_specs=in_specs,
        out_specs=out_specs,
        out_shape=out_shape,
        scratch_shapes=[pltpu.VMEM((tm, d), F32)],
        input_output_aliases=aliases,
        compiler_params=_params(("arbitrary", "arbitrary")),
        name="matmul_residual",
    )(*args)


def _ffn_act_kernel(*refs, width, vertical, aliased):
    if vertical:
        gm_ref, gp_ref, gn_ref, val_ref, w_ref, b_ref = refs[:6]
        rest = refs[6:]
    else:
        gm_ref, val_ref, w_ref, b_ref = refs[:4]
        rest = refs[4:]
    if aliased:
        rest = rest[1:]
    o_ref, s_scr = rest
    j = pl.program_id(1)
    if not vertical:
        gp_ref = gn_ref = None

    _act_fill(gm_ref, gp_ref, gn_ref, s_scr, j > 0, j < pl.num_programs(1) - 1, width=width, vertical=vertical)
    for r in range(o_ref.shape[0] // ACT_ROWS):
        for c in range(o_ref.shape[1] // 128):
            o_ref[r * ACT_ROWS:(r + 1) * ACT_ROWS, c * 128:(c + 1) * 128] = _act_rows(
                r, c, val_ref, w_ref, b_ref, s_scr, width=width, vertical=vertical).astype(o_ref.dtype)


def _act_fill(gm_ref, gp_ref, gn_ref, s_scr, has_above, has_below, *, width, vertical):
    tl, tf = gm_ref.shape
    pad = 8
    base = pad + (width if vertical else 0)
    s_scr[0:pad, :] = jnp.zeros((pad, tf), F32)
    s_scr[s_scr.shape[0] - pad:, :] = jnp.zeros((pad, tf), F32)
    s_scr[base:base + tl, :] = gm_ref[...].astype(F32)
    if vertical:
        s_scr[pad:pad + width, :] = jnp.where(has_above, gp_ref[...].astype(F32), 0.0)
        s_scr[base + tl:base + tl + width, :] = jnp.where(has_below, gn_ref[...].astype(F32), 0.0)


def _act_rows(r, c, val_ref, w_ref, b_ref, s_scr, *, width, vertical):
    rb, tf = ACT_ROWS, 128
    lanes = slice(c * 128, (c + 1) * 128)
    base = 8 + (width if vertical else 0)
    dys = (-1, 0, 1) if vertical else (0,)
    col = (lax.broadcasted_iota(jnp.int32, (rb, tf), 0) + r * rb) % width
    acc = jnp.zeros((rb, tf), F32)
    for dx in (-1, 0, 1):
        part = jnp.zeros((rb, tf), F32)
        for dy in dys:
            tap = (dy + 1) * 3 + (dx + 1)
            start = base + r * rb + dy * width + dx
            part = part + w_ref[tap:tap + 1, lanes] * s_scr[start:start + rb, lanes]
        if dx == -1:
            part = jnp.where(col != 0, part, 0.0)
        elif dx == 1:
            part = jnp.where(col != width - 1, part, 0.0)
        acc = acc + part
    conv = acc + b_ref[:, lanes]
    val = val_ref[r * rb:(r + 1) * rb, lanes].astype(F32)
    return conv * _sigmoid(conv) * val


def _ffn_act(gv, w9, b, layer, act, *, base_row, nseq, seq, width, vertical):
    t = gv.shape[0]
    dff = w9.shape[2]
    tf = _pick(dff, (512, 256, 128))
    tl = _pick(seq, (512, 256, 128)) if vertical else seq
    assert tl % width == 0 and base_row % tl == 0 and tl % ACT_ROWS == 0
    nt = seq // tl
    nf = dff // tf
    base_blk = base_row // tl
    wpt = tl // width
    last_w = t // width - 1
    main = lambda s, j, f: (base_blk + s * nt + j, f)
    in_specs = [pl.BlockSpec((tl, tf), main)]
    args = [gv]
    if vertical:
        in_specs += [
            pl.BlockSpec((width, tf), lambda s, j, f: (jnp.maximum((base_blk + s * nt + j) * wpt - 1, 0), f)),
            pl.BlockSpec((width, tf), lambda s, j, f: (jnp.minimum((base_blk + s * nt + j + 1) * wpt, last_w), f)),
        ]
        args += [gv, gv]
    in_specs += [
        pl.BlockSpec((tl, tf), lambda s, j, f: (base_blk + s * nt + j, nf + f)),
        pl.BlockSpec((None, 9, tf), lambda s, j, f: (layer, 0, f)),
        pl.BlockSpec((None, 1, tf), lambda s, j, f: (layer, 0, f)),
    ]
    args += [gv, w9, b]
    aliases = {}
    if act is not None:
        in_specs.append(pl.BlockSpec(memory_space=pl.ANY))
        args.append(act)
        aliases = {len(args) - 1: 0}
    rows_scr = tl + 16 + (2 * width if vertical else 0)
    return pl.pallas_call(
        functools.partial(_ffn_act_kernel, width=width, vertical=vertical, aliased=act is not None),
        grid=(nseq, nt, nf),
        in_specs=in_specs,
        out_specs=pl.BlockSpec((tl, tf), main),
        out_shape=jax.ShapeDtypeStruct((t, dff), BF16),
        scratch_shapes=[pltpu.VMEM((rows_scr, tf), F32)],
        input_output_aliases=aliases,
        compiler_params=_params(("arbitrary", "arbitrary", "arbitrary")),
        name="ffn_act",
    )(*args)


def _ffn_down_kernel(gm_ref, gp_ref, gn_ref, val_ref, w9_ref, b9_ref, wd_ref, x_ref, gate_ref, ng_ref, *rest,
                     nk, n_steps, tiles_per_image, final):
    if final:
        o_ref, s_scr, act_a, act_b, acc_scr = rest
    else:
        sh_ref, sc_ref, xo_ref, h_ref, s_scr, act_a, act_b, acc_scr = rest
    s = pl.program_id(0)
    j = (jnp.minimum(s, n_steps - 1) // nk) % tiles_per_image
    consumed_chunk = jnp.maximum(s - 1, 0) % nk
    slots = (act_a, act_b)
    n_rows = gm_ref.shape[0] // ACT_ROWS
    n_lane = gm_ref.shape[1] // 128
    n_cols = acc_scr.shape[1] // MXU_COLS
    tk = gm_ref.shape[1]
    k_parts = [slice(k0, min(k0 + MXU_K, tk)) for k0 in range(0, tk, MXU_K)]

    def step(produce_slot, consume_slot):
        def consume(n):
            ks, cols = k_parts[n // n_cols], slice((n % n_cols) * MXU_COLS, (n % n_cols + 1) * MXU_COLS)
            acc_scr[:, cols] += jnp.dot(slots[consume_slot][:, ks], wd_ref[ks, cols], preferred_element_type=F32)

        def produce(r, c):
            slots[produce_slot][r * ACT_ROWS:(r + 1) * ACT_ROWS, c * 128:(c + 1) * 128] = _act_rows(
                r, c, val_ref, w9_ref, b9_ref, s_scr, width=GRID_W, vertical=True).astype(BF16)

        pieces = [(r, c) for r in range(n_rows) for c in range(n_lane)] if produce_slot is not None else []
        n_consume = n_cols * len(k_parts) if consume_slot is not None else 0
        done = 0
        if n_consume:
            consume(0)
            done = 1
        if produce_slot is not None:
            _act_fill(gm_ref, gp_ref, gn_ref, s_scr, j > 0, j < tiles_per_image - 1, width=GRID_W, vertical=True)
        for idx, (r, c) in enumerate(pieces):
            produce(r, c)
            target = 1 + (idx + 1) * (n_consume - 1) // len(pieces) if n_consume else 0
            while done < target:
                consume(done)
                done += 1
        while done < n_consume:
            consume(done)
            done += 1

    @pl.when(s == 0)
    def _():
        acc_scr[...] = jnp.zeros(acc_scr.shape, F32)
        step(0, None)

    middle = jnp.logical_and(s > 0, s < n_steps)

    @pl.when(jnp.logical_and(middle, s % 2 == 0))
    def _():
        step(0, 1)

    @pl.when(jnp.logical_and(middle, s % 2 == 1))
    def _():
        step(1, 0)

    @pl.when(s == n_steps)
    def _():
        step(None, (n_steps - 1) % 2)

    @pl.when(jnp.logical_and(s > 0, consumed_chunk == nk - 1))
    def _():
        if final:
            _residual_epilogue(x_ref, acc_scr, gate_ref, ng_ref, None, None, None, None, o_ref, clear_acc=True)
        else:
            _residual_epilogue(x_ref, acc_scr, gate_ref, ng_ref, sh_ref, sc_ref, xo_ref, h_ref, None, clear_acc=True)


def _ffn_down_fused(gv, w9, b9, w_down, layer, x_all, mods, gate_which, norm_g, norm_layer, next_mods, *, n_lat, seq,
                    batch):
    t, d = x_all.shape
    dff = w9.shape[2]
    tk = _pick(dff, (1408, 512, 256, 128))
    tl = _pick(seq, (512, 256, 128))
    nk = dff // tk
    n_steps = (n_lat // tl) * nk
    assert tl % GRID_W == 0 and tk % 128 == 0
    wpt = tl // GRID_W
    last_w = t // GRID_W - 1
    final = next_mods is None

    def produced(s):
        sp = jnp.minimum(s, n_steps - 1)
        return sp // nk, sp % nk

    def consumed(s):
        sc = jnp.maximum(s - 1, 0)
        return sc // nk, sc % nk

    consumed_tile = lambda s: consumed(s)[0]
    in_specs = [
        pl.BlockSpec((tl, tk), lambda s: produced(s)),
        pl.BlockSpec((GRID_W, tk), lambda s: (jnp.maximum(produced(s)[0] * wpt - 1, 0), produced(s)[1])),
        pl.BlockSpec((GRID_W, tk), lambda s: (jnp.minimum((produced(s)[0] + 1) * wpt, last_w), produced(s)[1])),
        pl.BlockSpec((tl, tk), lambda s: (produced(s)[0], nk + produced(s)[1])),
        pl.BlockSpec((None, 9, tk), lambda s: (layer, 0, produced(s)[1])),
        pl.BlockSpec((None, 1, tk), lambda s: (layer, 0, produced(s)[1])),
        pl.BlockSpec((None, tk, d), lambda s: (layer, consumed(s)[1], 0)),
        pl.BlockSpec((tl, d), lambda s: (consumed_tile(s), 0)),
        _mod_spec(layer, gate_which, tl, seq, batch, d, tile_of=consumed_tile),
    ]
    args = [gv, gv, gv, gv, w9, b9, w_down, x_all, mods]
    row_spec = pl.BlockSpec((tl, d), lambda s: (consumed_tile(s), 0))
    if final:
        in_specs.append(pl.BlockSpec((1, d), lambda s: (0, 0)))
        args.append(norm_g.reshape(1, d))
        out_specs, out_shape, aliases = row_spec, jax.ShapeDtypeStruct((n_lat, d), F32), {}
    else:
        nl, shw, scw = next_mods
        in_specs += [
            pl.BlockSpec((None, 1, d), lambda s: (norm_layer, 0, 0)),
            _mod_spec(nl, shw, tl, seq, batch, d, tile_of=consumed_tile),
            _mod_spec(nl, scw, tl, seq, batch, d, tile_of=consumed_tile),
        ]
        args += [norm_g, mods, mods]
        out_specs = [row_spec, row_spec]
        out_shape = [jax.ShapeDtypeStruct((t, d), F32), jax.ShapeDtypeStruct((t, d), BF16)]
        aliases = {7: 0}
    return pl.pallas_call(
        functools.partial(_ffn_down_kernel, nk=nk, n_steps=n_steps, tiles_per_image=seq // tl, final=final),
        grid=(n_steps + 1,),
        in_specs=in_specs,
        out_specs=out_specs,
        out_shape=out_shape,
        scratch_shapes=[
            pltpu.VMEM((tl + 2 * GRID_W + 16, tk), F32),
            pltpu.VMEM((tl, tk), BF16),
            pltpu.VMEM((tl, tk), BF16),
            pltpu.VMEM((tl, d), F32),
        ],
        input_output_aliases=aliases,
        compiler_params=_params(("arbitrary",)),
        name="ffn_down_fused",
    )(*args)


def _rope_tables(seq):
    t = jnp.arange(seq)
    quarter = HEAD_DIM // 4
    inv_freq = 1.0 / (ROPE_THETA ** (jnp.arange(0, quarter, dtype=F32) / quarter))
    ang_r = (t // GRID_W).astype(F32)[:, None] * inv_freq[None, :]
    ang_c = (t % GRID_W).astype(F32)[:, None] * inv_freq[None, :]
    cos = jnp.concatenate([jnp.cos(ang_r)] * 2 + [jnp.cos(ang_c)] * 2, axis=-1)
    sin = jnp.concatenate([-jnp.sin(ang_r), jnp.sin(ang_r), -jnp.sin(ang_c), jnp.sin(ang_c)], axis=-1)
    return cos, sin


def kernel(x, c, ctx, c_ctx, w_mod, b_mod, norm1_g, norm2_g, w_in, ret_decay_f, ret_decay_b, conv_dw_w, conv_dw_b,
           conv_ln_g, conv_ln_b, w_out, ffn_w_up, ffn_dw_w, ffn_dw_b, ffn_w_down, final_norm_g):
    bsz, seq, d = x.shape
    lc = ctx.shape[1]
    depth = w_mod.shape[0]
    heads = ret_decay_f.shape[1]
    rw = heads * HEAD_DIM
    cw = conv_dw_w.shape[2]
    dff = ffn_dw_b.shape[1]
    n_lat, n_ctx = bsz * seq, bsz * lc
    t = n_lat + n_ctx
    assert bsz + 1 <= MOD_ROWS and rw % cw == 0 and seq % GRID_W == 0 and w_mod.shape[2] == N_MOD * d

    tm = _pick(_gcd(seq, n_ctx), (1024, 512, 256, 128))
    tm_res = min(tm, 512)
    tn_in = _pick(w_in.shape[2], (1024, 512, 256, 128))
    tn_up = _pick(2 * dff, (1024, 512, 256, 128))
    tk_down = _pick(dff, (1408, 1024, 512, 256, 128))

    cs = jnp.zeros((MOD_ROWS, d), F32).at[:bsz].set(c).at[bsz].set(c_ctx)
    mods = _modulation(cs, w_mod, b_mod).reshape(depth * MOD_ROWS * N_MOD, 1, d)

    w_in_b, w_out_b = w_in.astype(BF16), w_out.astype(BF16)
    w_up_b, w_down_b = ffn_w_up.astype(BF16), ffn_w_down.astype(BF16)
    cos, sin = _rope_tables(seq)
    dec = jnp.broadcast_to(jnp.stack([ret_decay_f, ret_decay_b], axis=1).astype(F32)[:, :, :, None, None],
                           (depth, 2, heads, 1, HEAD_DIM))
    n1 = norm1_g.reshape(depth, 1, d)
    n2 = norm2_g.reshape(depth, 1, d)
    cb = conv_dw_b.reshape(depth, 1, cw)
    clg = conv_ln_g.reshape(depth, 1, cw)
    clb = conv_ln_b.reshape(depth, 1, cw)
    w9 = ffn_dw_w.reshape(depth, 9, dff)
    fb = ffn_dw_b.reshape(depth, 1, dff)
    zero_state = jnp.zeros((bsz, heads, HEAD_DIM, HEAD_DIM), F32)

    tiles_all, tiles_lat = t // tm, n_lat // tm
    res_all, res_lat = t // tm_res, n_lat // tm_res
    mix_w = rw + cw
    col_a = 4 * rw // cw

    x_all, h = _first_norm(x.reshape(n_lat, d), ctx.reshape(n_ctx, d), n1, mods, 0, tm_res, seq, bsz)
    out = None
    for l in range(depth):
        last = l == depth - 1
        p = _projection(h, w_in_b, l, tiles_all, tm, tn_in)
        mix, s_f, s_b = _retention(p, dec, l, zero_state, zero_state, None, None, base_row=n_lat, nseq=bsz, seq=lc,
                                   heads=heads, mix_width=mix_w)
        mix, _, _ = _retention(p, dec, l, s_f, s_b, mix, (cos, sin), base_row=0, nseq=bsz, seq=seq, heads=heads,
                               mix_width=mix_w)
        mix = _conformer_conv(p, conv_dw_w, cb, clg, clb, l, mix, base_row=0, nseq=bsz, seq=seq, col_a=col_a,
                              col_out=rw // cw)
        if not last:
            mix = _conformer_conv(p, conv_dw_w, cb, clg, clb, l, mix, base_row=n_lat, nseq=bsz, seq=lc, col_a=col_a,
                                  col_out=rw // cw)
        x_all, h2 = _mm_residual(mix, w_out_b, l, x_all, mods, 2, n2, l, (l, 3, 4),
                                 n_tiles=res_lat if last else res_all, tm=tm_res, tk=mix_w, seq=seq, batch=bsz)
        gv = _projection(h2, w_up_b, l, tiles_lat if last else tiles_all, tm, tn_up)
        if not last:
            act_ctx = _ffn_act(gv, w9, fb, l, None, base_row=n_lat, nseq=bsz, seq=lc, width=lc, vertical=False)
            x_all, h = _ffn_down_fused(gv, w9, fb, w_down_b, l, x_all, mods, 5, n1, l + 1, (l + 1, 0, 1),
                                       n_lat=n_lat, seq=seq, batch=bsz)
            x_all, h = _mm_residual(act_ctx, w_down_b, l, x_all, mods, 5, n1, l + 1, (l + 1, 0, 1),
                                    n_tiles=res_all - res_lat, tm=tm_res, tk=tk_down, seq=seq, batch=bsz,
                                    tile0=res_lat, h_prev=h)
        else:
            out = _ffn_down_fused(gv, w9, fb, w_down_b, l, x_all, mods, 5, final_norm_g, 0, None,
                                  n_lat=n_lat, seq=seq, batch=bsz)
    return out.reshape(bsz, seq, d)


def _gcd(a, b):
    while b:
        a, b = b, a % b
    return a
```

```python
import functools

import jax
import jax.numpy as jnp
from jax import lax
from jax.experimental import pallas as pl
from jax.experimental.pallas import tpu as pltpu

F32 = jnp.float32
BF16 = jnp.bfloat16

GRID_W = 64
HEAD_DIM = 128
CHUNK = 128
N_MOD = 6
ROPE_THETA = 10000.0
EPS = 1e-6
MOD_ROWS = 8
CONV_HALO = 16
ACT_ROWS = 64
MXU_COLS = 256
MXU_K = 512
ROW_BLOCK = 16
LOOP_UNROLL = 8
VMEM_LIMIT = 52 * 1024 * 1024

ACT_DTYPE = BF16


def _pick(n, candidates):
    for c in candidates:
        if n % c == 0:
            return c
    raise ValueError(f"no tile for {n} in {candidates}")


def _params(sem):
    return pltpu.CompilerParams(dimension_semantics=sem, vmem_limit_bytes=VMEM_LIMIT)


def _sigmoid(x):
    return jax.nn.sigmoid(x)


def _norm_mod(x, g, sh, sc):
    xn = x * lax.rsqrt(jnp.mean(x * x, axis=-1, keepdims=True) + EPS)
    return (xn * g) * (1.0 + sc) + sh


def _mod_kernel(c_ref, w_ref, b_ref, o_ref):
    c = c_ref[...]
    s = c * _sigmoid(c)
    o_ref[...] = jnp.dot(s.astype(BF16), w_ref[...].astype(BF16), preferred_element_type=F32) + b_ref[...]


def _modulation(cs, w_mod, b_mod):
    depth, d, n = w_mod.shape
    tn = _pick(n, (1024, 512, 256, 128))
    return pl.pallas_call(
        _mod_kernel,
        grid=(depth, n // tn),
        in_specs=[
            pl.BlockSpec((MOD_ROWS, d), lambda l, j: (0, 0)),
            pl.BlockSpec((None, d, tn), lambda l, j: (l, 0, j)),
            pl.BlockSpec((None, 1, tn), lambda l, j: (l, 0, j)),
        ],
        out_specs=pl.BlockSpec((None, MOD_ROWS, tn), lambda l, j: (l, 0, j)),
        out_shape=jax.ShapeDtypeStruct((depth, MOD_ROWS, n), F32),
        compiler_params=_params(("arbitrary", "arbitrary")),
        name="modulation",
    )(cs, w_mod, b_mod.reshape(depth, 1, n))


def _mod_spec(layer, which, tm, seq, batch, d, tile0=0, tile_of=None):
    def imap(i, *rest):
        tile = (i + tile0) if tile_of is None else tile_of(i, *rest)
        grp = jnp.minimum((tile * tm) // seq, batch)
        return ((layer * MOD_ROWS + grp) * N_MOD + which, 0, 0)
    return pl.BlockSpec((None, 1, d), imap)


def _first_norm_kernel(xl_ref, xc_ref, g_ref, sh_ref, sc_ref, x_ref, h_ref, *, n_lat_tiles):
    g, sh, sc = g_ref[...], sh_ref[...], sc_ref[...]

    def run(src_ref):
        def body(r, carry):
            rows = pl.ds(pl.multiple_of(r * ROW_BLOCK, ROW_BLOCK), ROW_BLOCK)
            xr = src_ref[rows, :]
            x_ref[rows, :] = xr
            h_ref[rows, :] = _norm_mod(xr, g, sh, sc).astype(h_ref.dtype)
            return carry

        lax.fori_loop(0, x_ref.shape[0] // ROW_BLOCK, body, 0, unroll=LOOP_UNROLL)

    @pl.when(pl.program_id(0) < n_lat_tiles)
    def _():
        run(xl_ref)

    @pl.when(pl.program_id(0) >= n_lat_tiles)
    def _():
        run(xc_ref)


def _first_norm(x_lat, x_ctx, norm_g, mods, layer, tm, seq, batch):
    n_lat, d = x_lat.shape
    n_ctx = x_ctx.shape[0]
    t = n_lat + n_ctx
    n_lat_tiles, n_ctx_tiles = n_lat // tm, n_ctx // tm
    return pl.pallas_call(
        functools.partial(_first_norm_kernel, n_lat_tiles=n_lat_tiles),
        grid=(t // tm,),
        in_specs=[
            pl.BlockSpec((tm, d), lambda i: (jnp.minimum(i, n_lat_tiles - 1), 0)),
            pl.BlockSpec((tm, d), lambda i: (jnp.clip(i - n_lat_tiles, 0, n_ctx_tiles - 1), 0)),
            pl.BlockSpec((None, 1, d), lambda i: (layer, 0, 0)),
            _mod_spec(layer, 0, tm, seq, batch, d),
            _mod_spec(layer, 1, tm, seq, batch, d),
        ],
        out_specs=[pl.BlockSpec((tm, d), lambda i: (i, 0)), pl.BlockSpec((tm, d), lambda i: (i, 0))],
        out_shape=[jax.ShapeDtypeStruct((t, d), F32), jax.ShapeDtypeStruct((t, d), BF16)],
        compiler_params=_params(("arbitrary",)),
        name="first_norm",
    )(x_lat, x_ctx, norm_g, mods, mods)


def _proj_kernel(a_ref, w_ref, o_ref):
    o_ref[...] = jnp.dot(a_ref[...], w_ref[...], preferred_element_type=F32).astype(o_ref.dtype)


def _projection(a, w, layer, n_tiles, tm, tn):
    t, kdim = a.shape
    n = w.shape[2]
    return pl.pallas_call(
        _proj_kernel,
        grid=(n_tiles, n // tn),
        in_specs=[
            pl.BlockSpec((tm, kdim), lambda i, j: (i, 0)),
            pl.BlockSpec((None, kdim, tn), lambda i, j: (layer, 0, j)),
        ],
        out_specs=pl.BlockSpec((tm, tn), lambda i, j: (i, j)),
        out_shape=jax.ShapeDtypeStruct((t, n), ACT_DTYPE),
        compiler_params=_params(("arbitrary", "arbitrary")),
        name="projection",
    )(a, w)


def _log_sigmoid(x):
    return jnp.minimum(x, 0.0) - jnp.log1p(jnp.exp(-jnp.abs(x)))


def _rope(x, cos, sin_signed):
    lane = lax.broadcasted_iota(jnp.int32, x.shape, 1)
    first = (lane % (HEAD_DIM // 2)) < (HEAD_DIM // 4)
    partner = jnp.where(first, pltpu.roll(x, HEAD_DIM - HEAD_DIM // 4, 1), pltpu.roll(x, HEAD_DIM // 4, 1))
    return x * cos + partner * sin_signed


def _retention_kernel(dec_ref, q_ref, k_ref, v_ref, g_ref, s0f_ref, s0b_ref, *rest, nseg, nchunk, aliased, rope):
    if rope:
        cos_ref, sin_ref = rest[:2]
        rest = rest[2:]
    if aliased:
        rest = rest[1:]
    o_ref, sf_ref, sb_ref, of_scr, state_scr, decay_scr = rest
    s = pl.program_id(2)
    c_len = CHUNK
    lseg = nchunk * c_len
    scale = HEAD_DIM ** -0.5

    @pl.when(s == 0)
    def _init():
        row = lax.broadcasted_iota(jnp.int32, (c_len, HEAD_DIM), 0).astype(F32)
        col = lax.broadcasted_iota(jnp.int32, (c_len, HEAD_DIM), 1).astype(F32)
        for d in range(2):
            lg = _log_sigmoid(dec_ref[d])
            diff = (row - col) if d == 0 else (col - row)
            decay_scr[d, 0] = jnp.where(diff >= 0, jnp.exp(lg * jnp.maximum(diff, 0.0)), 0.0)
            if d == 0:
                decay_scr[d, 1] = jnp.exp(lg * (row + 1.0))
                decay_scr[d, 2] = jnp.exp(lg * (c_len - 1.0 - row))
            else:
                decay_scr[d, 1] = jnp.exp(lg * (c_len - row))
                decay_scr[d, 2] = jnp.exp(lg * row)
            decay_scr[d, 3] = jnp.broadcast_to(jnp.exp(lg * float(c_len)), (c_len, HEAD_DIM))
        state_scr[...] = s0f_ref[...]

    @pl.when(s == nseg)
    def _init_b():
        state_scr[...] = s0b_ref[...]

    def chunk(c, d, state):
        rows = slice(c * c_len, (c + 1) * c_len)
        qf = q_ref[rows, :].astype(F32)
        kf = k_ref[rows, :].astype(F32)
        if rope:
            cos, sin = cos_ref[rows, :], sin_ref[rows, :]
            qf, kf = _rope(qf, cos, sin), _rope(kf, cos, sin)
        q = qf.astype(BF16)
        kf = kf * scale
        v = v_ref[rows, :].astype(BF16)
        sc = lax.dot_general(q, kf.astype(BF16), (((1,), (1,)), ((), ())), preferred_element_type=F32)
        sc = sc * decay_scr[d, 0]
        o = jnp.dot(sc.astype(BF16), v, preferred_element_type=F32)
        o = o + jnp.dot(q, state.astype(BF16), preferred_element_type=F32) * decay_scr[d, 1]
        kd = (kf * decay_scr[d, 2]).astype(BF16)
        state = state * decay_scr[d, 3] + lax.dot_general(kd, v, (((0,), (0,)), ((), ())), preferred_element_type=F32)
        return o, state

    @pl.when(s < nseg)
    def _fwd():
        state = state_scr[...]
        for c in range(nchunk):
            o, state = chunk(c, 0, state)
            of_scr[pl.ds(pl.multiple_of(s * lseg + c * c_len, c_len), c_len), :] = o
        state_scr[...] = state

        @pl.when(s == nseg - 1)
        def _():
            sf_ref[...] = state

    @pl.when(s >= nseg)
    def _bwd():
        seg = 2 * nseg - 1 - s
        state = state_scr[...]
        for c in reversed(range(nchunk)):
            o, state = chunk(c, 1, state)
            o = o + of_scr[pl.ds(pl.multiple_of(seg * lseg + c * c_len, c_len), c_len), :]
            o = o * lax.rsqrt(jnp.mean(o * o, axis=-1, keepdims=True) + EPS)
            g = g_ref[c * c_len:(c + 1) * c_len, :].astype(F32)
            o_ref[c * c_len:(c + 1) * c_len, :] = (g * _sigmoid(g) * o).astype(o_ref.dtype)
        state_scr[...] = state

        @pl.when(s == 2 * nseg - 1)
        def _():
            sb_ref[...] = state


def _retention(p, dec, layer, s0f, s0b, mix, rope, *, base_row, nseq, seq, heads, mix_width):
    t = p.shape[0]
    lseg = _pick(seq, (4096, 2048, 1024, 512, 256, 128))
    nseg = seq // lseg
    nchunk = lseg // CHUNK
    base_blk = base_row // lseg
    assert base_row % lseg == 0

    def seg_of(s):
        return jnp.where(s < nseg, s, 2 * nseg - 1 - s)

    def out_seg_of(s):
        return jnp.where(s < nseg, nseg - 1, 2 * nseg - 1 - s)

    def in_spec(col0):
        return pl.BlockSpec((lseg, HEAD_DIM), lambda b, h, s: (base_blk + b * nseg + seg_of(s), col0 + h))

    state_spec = pl.BlockSpec((None, None, HEAD_DIM, HEAD_DIM), lambda b, h, s: (b, h, 0, 0))
    in_specs = [
        pl.BlockSpec((None, 2, None, 1, HEAD_DIM), lambda b, h, s: (layer, 0, h, 0, 0)),
        in_spec(0), in_spec(heads), in_spec(2 * heads),
        pl.BlockSpec((lseg, HEAD_DIM), lambda b, h, s: (base_blk + b * nseg + out_seg_of(s), 3 * heads + h)),
        state_spec, state_spec,
    ]
    args = [dec, p, p, p, p, s0f, s0b]
    if rope is not None:
        in_specs += [pl.BlockSpec((lseg, HEAD_DIM), lambda b, h, s: (seg_of(s), 0))] * 2
        args += list(rope)
    aliases = {}
    if mix is not None:
        in_specs.append(pl.BlockSpec(memory_space=pl.ANY))
        args.append(mix)
        aliases = {len(args) - 1: 0}
    state_shape = jax.ShapeDtypeStruct((nseq, heads, HEAD_DIM, HEAD_DIM), F32)
    return pl.pallas_call(
        functools.partial(_retention_kernel, nseg=nseg, nchunk=nchunk, aliased=mix is not None,
                          rope=rope is not None),
        grid=(nseq, heads, 2 * nseg),
        in_specs=in_specs,
        out_specs=[
            pl.BlockSpec((lseg, HEAD_DIM), lambda b, h, s: (base_blk + b * nseg + out_seg_of(s), h)),
            state_spec, state_spec,
        ],
        out_shape=[jax.ShapeDtypeStruct((t, mix_width), BF16), state_shape, state_shape],
        scratch_shapes=[
            pltpu.VMEM((seq, HEAD_DIM), F32),
            pltpu.VMEM((HEAD_DIM, HEAD_DIM), F32),
            pltpu.VMEM((2, 4, CHUNK, HEAD_DIM), F32),
        ],
        input_output_aliases=aliases,
        compiler_params=_params(("arbitrary", "arbitrary", "arbitrary")),
        name="retention",
    )(*args)


def _cconv_kernel(am_ref, ap_ref, an_ref, bm_ref, bp_ref, bn_ref, w_ref, b_ref, lg_ref, lb_ref, mix_ref,
                  o_ref, u_scr, y_scr, us_scr, *, taps):
    del mix_ref
    j = pl.program_id(1)
    nj = pl.num_programs(1)
    tl, cw = o_ref.shape
    halo = CONV_HALO

    def glu(a_ref, b_ref_):
        return a_ref[...].astype(F32) * _sigmoid(b_ref_[...].astype(F32))

    u_scr[halo:halo + tl, :] = glu(am_ref, bm_ref)
    u_scr[0:halo, :] = jnp.where(j > 0, glu(ap_ref, bp_ref), 0.0)
    u_scr[halo + tl:halo + tl + halo, :] = jnp.where(j < nj - 1, glu(an_ref, bn_ref), 0.0)

    off0 = halo - taps // 2
    rb = 64 if tl % 64 == 0 else tl

    n_ext = us_scr.shape[1]

    def lane_body(c, carry):
        lanes = pl.ds(pl.multiple_of(c * 128, 128), 128)
        for ph in range(1, 8):
            us_scr[ph - 1] = u_scr[ph:ph + n_ext, lanes]
        for r in range(tl // rb):
            acc = jnp.zeros((rb, 128), F32)
            for k in range(taps):
                ph, a0 = (off0 + k) % 8, r * rb + 8 * ((off0 + k) // 8)
                src = u_scr[a0:a0 + rb, lanes] if ph == 0 else us_scr[ph - 1, a0:a0 + rb, :]
                acc = acc + w_ref[k:k + 1, lanes] * src
            y_scr[r * rb:(r + 1) * rb, lanes] = acc
        return carry

    lax.fori_loop(0, cw // 128, lane_body, 0)

    bias, ln_g, ln_b = b_ref[...], lg_ref[...], lb_ref[...]

    def row_body(r, carry):
        rows = pl.ds(pl.multiple_of(r * ROW_BLOCK, ROW_BLOCK), ROW_BLOCK)
        y = y_scr[rows, :] + bias
        mu = jnp.mean(y, axis=-1, keepdims=True)
        yc = y - mu
        var = jnp.mean(yc * yc, axis=-1, keepdims=True)
        z = yc * lax.rsqrt(var + EPS) * ln_g + ln_b
        o_ref[rows, :] = (z * _sigmoid(z)).astype(o_ref.dtype)
        return carry

    lax.fori_loop(0, tl // ROW_BLOCK, row_body, 0, unroll=LOOP_UNROLL)


def _conformer_conv(p, w, b, ln_g, ln_b, layer, mix, *, base_row, nseq, seq, col_a, col_out):
    t = p.shape[0]
    taps, cw = w.shape[1], w.shape[2]
    tl = _pick(seq, (512, 256, 128))
    nt = seq // tl
    assert base_row % tl == 0 and taps // 2 < CONV_HALO
    base_blk = base_row // tl
    hpt = tl // CONV_HALO
    last_halo = t // CONV_HALO - 1

    def main_spec(col):
        return pl.BlockSpec((tl, cw), lambda s, j: (base_blk + s * nt + j, col))

    def prev_spec(col):
        return pl.BlockSpec((CONV_HALO, cw), lambda s, j: (jnp.maximum((base_blk + s * nt + j) * hpt - 1, 0), col))

    def next_spec(col):
        return pl.BlockSpec((CONV_HALO, cw), lambda s, j: (jnp.minimum((base_blk + s * nt + j + 1) * hpt, last_halo), col))

    vec_spec = pl.BlockSpec((None, 1, cw), lambda s, j: (layer, 0, 0))
    return pl.pallas_call(
        functools.partial(_cconv_kernel, taps=taps),
        grid=(nseq, nt),
        in_specs=[
            main_spec(col_a), prev_spec(col_a), next_spec(col_a),
            main_spec(col_a + 1), prev_spec(col_a + 1), next_spec(col_a + 1),
            pl.BlockSpec((None, taps, cw), lambda s, j: (layer, 0, 0)),
            vec_spec, vec_spec, vec_spec,
            pl.BlockSpec(memory_space=pl.ANY),
        ],
        out_specs=pl.BlockSpec((tl, cw), lambda s, j: (base_blk + s * nt + j, col_out)),
        out_shape=jax.ShapeDtypeStruct(mix.shape, mix.dtype),
        scratch_shapes=[pltpu.VMEM((tl + 2 * CONV_HALO, cw), F32), pltpu.VMEM((tl, cw), F32),
                        pltpu.VMEM((7, tl + 8 * ((CONV_HALO + taps // 2) // 8), 128), F32)],
        input_output_aliases={10: 0},
        compiler_params=_params(("arbitrary", "arbitrary")),
        name="conformer_conv",
    )(p, p, p, p, p, p, w, b, ln_g, ln_b, mix)


def _residual_epilogue(x_ref, acc_scr, gate_ref, ng_ref, sh_ref, sc_ref, xo_ref, h_ref, o_ref, clear_acc=False):
    final = o_ref is not None
    gate, ng = gate_ref[...], ng_ref[...]
    if not final:
        sh, sc = sh_ref[...], sc_ref[...]

    def body(r, carry):
        rows = pl.ds(pl.multiple_of(r * ROW_BLOCK, ROW_BLOCK), ROW_BLOCK)
        xn = x_ref[rows, :] + gate * acc_scr[rows, :]
        if clear_acc:
            acc_scr[rows, :] = jnp.zeros((ROW_BLOCK, acc_scr.shape[1]), F32)
        if final:
            o_ref[rows, :] = xn * lax.rsqrt(jnp.mean(xn * xn, axis=-1, keepdims=True) + EPS) * ng
        else:
            xo_ref[rows, :] = xn
            h_ref[rows, :] = _norm_mod(xn, ng, sh, sc).astype(h_ref.dtype)
        return carry

    lax.fori_loop(0, x_ref.shape[0] // ROW_BLOCK, body, 0, unroll=LOOP_UNROLL)


def _mm_res_kernel(a_ref, w_ref, x_ref, gate_ref, ng_ref, *rest, final, h_aliased):
    if final:
        o_ref, acc_scr = rest
    else:
        sh_ref, sc_ref = rest[:2]
        xo_ref, h_ref, acc_scr = rest[3:] if h_aliased else rest[2:]
    k = pl.program_id(1)
    nk = pl.num_programs(1)
    prod = jnp.dot(a_ref[...], w_ref[...], preferred_element_type=F32)

    @pl.when(k == 0)
    def _():
        acc_scr[...] = prod

    @pl.when(k > 0)
    def _():
        acc_scr[...] += prod

    @pl.when(k == nk - 1)
    def _():
        if final:
            _residual_epilogue(x_ref, acc_scr, gate_ref, ng_ref, None, None, None, None, o_ref)
        else:
            _residual_epilogue(x_ref, acc_scr, gate_ref, ng_ref, sh_ref, sc_ref, xo_ref, h_ref, None)


def _mm_residual(a, w, layer, x_all, mods, gate_which, norm_g, norm_layer, next_mods, *, n_tiles, tm, tk, seq, batch,
                 final_rows=None, tile0=0, h_prev=None):
    t, d = x_all.shape
    kdim = a.shape[1]
    final = next_mods is None
    in_specs = [
        pl.BlockSpec((tm, tk), lambda i, k: (i + tile0, k)),
        pl.BlockSpec((None, tk, d), lambda i, k: (layer, k, 0)),
        pl.BlockSpec((tm, d), lambda i, k: (i + tile0, 0)),
        _mod_spec(layer, gate_which, tm, seq, batch, d, tile0),
    ]
    args = [a, w, x_all, mods]
    if final:
        in_specs.append(pl.BlockSpec((1, d), lambda i, k: (0, 0)))
        args.append(norm_g.reshape(1, d))
        out_specs = pl.BlockSpec((tm, d), lambda i, k: (i + tile0, 0))
        out_shape = jax.ShapeDtypeStruct((final_rows, d), F32)
        aliases = {}
    else:
        nl, shw, scw = next_mods
        in_specs += [
            pl.BlockSpec((None, 1, d), lambda i, k: (norm_layer, 0, 0)),
            _mod_spec(nl, shw, tm, seq, batch, d, tile0),
            _mod_spec(nl, scw, tm, seq, batch, d, tile0),
        ]
        args += [norm_g, mods, mods]
        row_spec = pl.BlockSpec((tm, d), lambda i, k: (i + tile0, 0))
        out_specs = [row_spec, row_spec]
        out_shape = [jax.ShapeDtypeStruct((t, d), F32), jax.ShapeDtypeStruct((t, d), BF16)]
        aliases = {2: 0}
        if h_prev is not None:
            in_specs.append(pl.BlockSpec(memory_space=pl.ANY))
            args.append(h_prev)
            aliases[len(args) - 1] = 1
    return pl.pallas_call(
        functools.partial(_mm_res_kernel, final=final, h_aliased=h_prev is not None),
        grid=(n_tiles, kdim // tk),
        in_specs=in_specs,
        out_specs=out_specs,
        out_shape=out_shape,
        scratch_shapes=[pltpu.VMEM((tm, d), F32)],
        input_output_aliases=aliases,
        compiler_params=_params(("arbitrary", "arbitrary")),
        name="matmul_residual",
    )(*args)


def _ffn_act_kernel(*refs, width, vertical, aliased):
    if vertical:
        gm_ref, gp_ref, gn_ref, val_ref, w_ref, b_ref = refs[:6]
        rest = refs[6:]
    else:
        gm_ref, val_ref, w_ref, b_ref = refs[:4]
        rest = refs[4:]
    if aliased:
        rest = rest[1:]
    o_ref, s_scr = rest
    j = pl.program_id(1)
    if not vertical:
        gp_ref = gn_ref = None

    _act_fill(gm_ref, gp_ref, gn_ref, s_scr, j > 0, j < pl.num_programs(1) - 1, width=width, vertical=vertical)
    for r in range(o_ref.shape[0] // ACT_ROWS):
        for c in range(o_ref.shape[1] // 128):
            o_ref[r * ACT_ROWS:(r + 1) * ACT_ROWS, c * 128:(c + 1) * 128] = _act_rows(
                r, c, val_ref, w_ref, b_ref, s_scr, width=width, vertical=vertical).astype(o_ref.dtype)


def _act_fill(gm_ref, gp_ref, gn_ref, s_scr, has_above, has_below, *, width, vertical):
    tl, tf = gm_ref.shape
    pad = 8
    base = pad + (width if vertical else 0)
    s_scr[0:pad, :] = jnp.zeros((pad, tf), F32)
    s_scr[s_scr.shape[0] - pad:, :] = jnp.zeros((pad, tf), F32)
    s_scr[base:base + tl, :] = gm_ref[...].astype(F32)
    if vertical:
        s_scr[pad:pad + width, :] = jnp.where(has_above, gp_ref[...].astype(F32), 0.0)
        s_scr[base + tl:base + tl + width, :] = jnp.where(has_below, gn_ref[...].astype(F32), 0.0)


def _act_rows(r, c, val_ref, w_ref, b_ref, s_scr, *, width, vertical):
    rb, tf = ACT_ROWS, 128
    lanes = slice(c * 128, (c + 1) * 128)
    base = 8 + (width if vertical else 0)
    dys = (-1, 0, 1) if vertical else (0,)
    col = (lax.broadcasted_iota(jnp.int32, (rb, tf), 0) + r * rb) % width
    acc = jnp.zeros((rb, tf), F32)
    for dx in (-1, 0, 1):
        part = jnp.zeros((rb, tf), F32)
        for dy in dys:
            tap = (dy + 1) * 3 + (dx + 1)
            start = base + r * rb + dy * width + dx
            part = part + w_ref[tap:tap + 1, lanes] * s_scr[start:start + rb, lanes]
        if dx == -1:
            part = jnp.where(col != 0, part, 0.0)
        elif dx == 1:
            part = jnp.where(col != width - 1, part, 0.0)
        acc = acc + part
    conv = acc + b_ref[:, lanes]
    val = val_ref[r * rb:(r + 1) * rb, lanes].astype(F32)
    return conv * _sigmoid(conv) * val


def _ffn_act(gv, w9, b, layer, act, *, base_row, nseq, seq, width, vertical):
    t = gv.shape[0]
    dff = w9.shape[2]
    tf = _pick(dff, (512, 256, 128))
    tl = _pick(seq, (512, 256, 128)) if vertical else seq
    assert tl % width == 0 and base_row % tl == 0 and tl % ACT_ROWS == 0
    nt = seq // tl
    nf = dff // tf
    base_blk = base_row // tl
    wpt = tl // width
    last_w = t // width - 1
    main = lambda s, j, f: (base_blk + s * nt + j, f)
    in_specs = [pl.BlockSpec((tl, tf), main)]
    args = [gv]
    if vertical:
        in_specs += [
            pl.BlockSpec((width, tf), lambda s, j, f: (jnp.maximum((base_blk + s * nt + j) * wpt - 1, 0), f)),
            pl.BlockSpec((width, tf), lambda s, j, f: (jnp.minimum((base_blk + s * nt + j + 1) * wpt, last_w), f)),
        ]
        args += [gv, gv]
    in_specs += [
        pl.BlockSpec((tl, tf), lambda s, j, f: (base_blk + s * nt + j, nf + f)),
        pl.BlockSpec((None, 9, tf), lambda s, j, f: (layer, 0, f)),
        pl.BlockSpec((None, 1, tf), lambda s, j, f: (layer, 0, f)),
    ]
    args += [gv, w9, b]
    aliases = {}
    if act is not None:
        in_specs.append(pl.BlockSpec(memory_space=pl.ANY))
        args.append(act)
        aliases = {len(args) - 1: 0}
    rows_scr = tl + 16 + (2 * width if vertical else 0)
    return pl.pallas_call(
        functools.partial(_ffn_act_kernel, width=width, vertical=vertical, aliased=act is not None),
        grid=(nseq, nt, nf),
        in_specs=in_specs,
        out_specs=pl.BlockSpec((tl, tf), main),
        out_shape=jax.ShapeDtypeStruct((t, dff), BF16),
        scratch_shapes=[pltpu.VMEM((rows_scr, tf), F32)],
        input_output_aliases=aliases,
        compiler_params=_params(("arbitrary", "arbitrary", "arbitrary")),
        name="ffn_act",
    )(*args)


def _ffn_down_kernel(gm_ref, gp_ref, gn_ref, val_ref, w9_ref, b9_ref, wd_ref, x_ref, gate_ref, ng_ref, *rest,
                     nk, n_steps, tiles_per_image, final):
    if final:
        o_ref, s_scr, act_a, act_b, acc_scr = rest
    else:
        sh_ref, sc_ref, xo_ref, h_ref, s_scr, act_a, act_b, acc_scr = rest
    s = pl.program_id(0)
    j = (jnp.minimum(s, n_steps - 1) // nk) % tiles_per_image
    consumed_chunk = jnp.maximum(s - 1, 0) % nk
    slots = (act_a, act_b)
    n_rows = gm_ref.shape[0] // ACT_ROWS
    n_lane = gm_ref.shape[1] // 128
    n_cols = acc_scr.shape[1] // MXU_COLS
    tk = gm_ref.shape[1]
    k_parts = [slice(k0, min(k0 + MXU_K, tk)) for k0 in range(0, tk, MXU_K)]

    def step(produce_slot, consume_slot):
        def consume(n):
            ks, cols = k_parts[n // n_cols], slice((n % n_cols) * MXU_COLS, (n % n_cols + 1) * MXU_COLS)
            acc_scr[:, cols] += jnp.dot(slots[consume_slot][:, ks], wd_ref[ks, cols], preferred_element_type=F32)

        def produce(r, c):
            slots[produce_slot][r * ACT_ROWS:(r + 1) * ACT_ROWS, c * 128:(c + 1) * 128] = _act_rows(
                r, c, val_ref, w9_ref, b9_ref, s_scr, width=GRID_W, vertical=True).astype(BF16)

        pieces = [(r, c) for r in range(n_rows) for c in range(n_lane)] if produce_slot is not None else []
        n_consume = n_cols * len(k_parts) if consume_slot is not None else 0
        done = 0
        if n_consume:
            consume(0)
            done = 1
        if produce_slot is not None:
            _act_fill(gm_ref, gp_ref, gn_ref, s_scr, j > 0, j < tiles_per_image - 1, width=GRID_W, vertical=True)
        for idx, (r, c) in enumerate(pieces):
            produce(r, c)
            target = 1 + (idx + 1) * (n_consume - 1) // len(pieces) if n_consume else 0
            while done < target:
                consume(done)
                done += 1
        while done < n_consume:
            consume(done)
            done += 1

    @pl.when(s == 0)
    def _():
        acc_scr[...] = jnp.zeros(acc_scr.shape, F32)
        step(0, None)

    middle = jnp.logical_and(s > 0, s < n_steps)

    @pl.when(jnp.logical_and(middle, s % 2 == 0))
    def _():
        step(0, 1)

    @pl.when(jnp.logical_and(middle, s % 2 == 1))
    def _():
        step(1, 0)

    @pl.when(s == n_steps)
    def _():
        step(None, (n_steps - 1) % 2)

    @pl.when(jnp.logical_and(s > 0, consumed_chunk == nk - 1))
    def _():
        if final:
            _residual_epilogue(x_ref, acc_scr, gate_ref, ng_ref, None, None, None, None, o_ref, clear_acc=True)
        else:
            _residual_epilogue(x_ref, acc_scr, gate_ref, ng_ref, sh_ref, sc_ref, xo_ref, h_ref, None, clear_acc=True)


def _ffn_down_fused(gv, w9, b9, w_down, layer, x_all, mods, gate_which, norm_g, norm_layer, next_mods, *, n_lat, seq,
                    batch):
    t, d = x_all.shape
    dff = w9.shape[2]
    tk = _pick(dff, (1408, 512, 256, 128))
    tl = _pick(seq, (512, 256, 128))
    nk = dff // tk
    n_steps = (n_lat // tl) * nk
    assert tl % GRID_W == 0 and tk % 128 == 0
    wpt = tl // GRID_W
    last_w = t // GRID_W - 1
    final = next_mods is None

    def produced(s):
        sp = jnp.minimum(s, n_steps - 1)
        return sp // nk, sp % nk

    def consumed(s):
        sc = jnp.maximum(s - 1, 0)
        return sc // nk, sc % nk

    consumed_tile = lambda s: consumed(s)[0]
    in_specs = [
        pl.BlockSpec((tl, tk), lambda s: produced(s)),
        pl.BlockSpec((GRID_W, tk), lambda s: (jnp.maximum(produced(s)[0] * wpt - 1, 0), produced(s)[1])),
        pl.BlockSpec((GRID_W, tk), lambda s: (jnp.minimum((produced(s)[0] + 1) * wpt, last_w), produced(s)[1])),
        pl.BlockSpec((tl, tk), lambda s: (produced(s)[0], nk + produced(s)[1])),
        pl.BlockSpec((None, 9, tk), lambda s: (layer, 0, produced(s)[1])),
        pl.BlockSpec((None, 1, tk), lambda s: (layer, 0, produced(s)[1])),
        pl.BlockSpec((None, tk, d), lambda s: (layer, consumed(s)[1], 0)),
        pl.BlockSpec((tl, d), lambda s: (consumed_tile(s), 0)),
        _mod_spec(layer, gate_which, tl, seq, batch, d, tile_of=consumed_tile),
    ]
    args = [gv, gv, gv, gv, w9, b9, w_down, x_all, mods]
    row_spec = pl.BlockSpec((tl, d), lambda s: (consumed_tile(s), 0))
    if final:
        in_specs.append(pl.BlockSpec((1, d), lambda s: (0, 0)))
        args.append(norm_g.reshape(1, d))
        out_specs, out_shape, aliases = row_spec, jax.ShapeDtypeStruct((n_lat, d), F32), {}
    else:
        nl, shw, scw = next_mods
        in_specs += [
            pl.BlockSpec((None, 1, d), lambda s: (norm_layer, 0, 0)),
            _mod_spec(nl, shw, tl, seq, batch, d, tile_of=consumed_tile),
            _mod_spec(nl, scw, tl, seq, batch, d, tile_of=consumed_tile),
        ]
        args += [norm_g, mods, mods]
        out_specs = [row_spec, row_spec]
        out_shape = [jax.ShapeDtypeStruct((t, d), F32), jax.ShapeDtypeStruct((t, d), BF16)]
        aliases = {7: 0}
    return pl.pallas_call(
        functools.partial(_ffn_down_kernel, nk=nk, n_steps=n_steps, tiles_per_image=seq // tl, final=final),
        grid=(n_steps + 1,),
        in_specs=in_specs,
        out_specs=out_specs,
        out_shape=out_shape,
        scratch_shapes=[
            pltpu.VMEM((tl + 2 * GRID_W + 16, tk), F32),
            pltpu.VMEM((tl, tk), BF16),
            pltpu.VMEM((tl, tk), BF16),
            pltpu.VMEM((tl, d), F32),
        ],
        input_output_aliases=aliases,
        compiler_params=_params(("arbitrary",)),
        name="ffn_down_fused",
    )(*args)


def _rope_tables(seq):
    t = jnp.arange(seq)
    quarter = HEAD_DIM // 4
    inv_freq = 1.0 / (ROPE_THETA ** (jnp.arange(0, quarter, dtype=F32) / quarter))
    ang_r = (t // GRID_W).astype(F32)[:, None] * inv_freq[None, :]
    ang_c = (t % GRID_W).astype(F32)[:, None] * inv_freq[None, :]
    cos = jnp.concatenate([jnp.cos(ang_r)] * 2 + [jnp.cos(ang_c)] * 2, axis=-1)
    sin = jnp.concatenate([-jnp.sin(ang_r), jnp.sin(ang_r), -jnp.sin(ang_c), jnp.sin(ang_c)], axis=-1)
    return cos, sin


def kernel(x, c, ctx, c_ctx, w_mod, b_mod, norm1_g, norm2_g, w_in, ret_decay_f, ret_decay_b, conv_dw_w, conv_dw_b,
           conv_ln_g, conv_ln_b, w_out, ffn_w_up, ffn_dw_w, ffn_dw_b, ffn_w_down, final_norm_g):
    bsz, seq, d = x.shape
    lc = ctx.shape[1]
    depth = w_mod.shape[0]
    heads = ret_decay_f.shape[1]
    rw = heads * HEAD_DIM
    cw = conv_dw_w.shape[2]
    dff = ffn_dw_b.shape[1]
    n_lat, n_ctx = bsz * seq, bsz * lc
    t = n_lat + n_ctx
    assert bsz + 1 <= MOD_ROWS and rw % cw == 0 and seq % GRID_W == 0 and w_mod.shape[2] == N_MOD * d

    tm = _pick(_gcd(seq, n_ctx), (1024, 512, 256, 128))
    tm_res = min(tm, 512)
    tn_in = _pick(w_in.shape[2], (1024, 512, 256, 128))
    tn_up = _pick(2 * dff, (1024, 512, 256, 128))
    tk_down = _pick(dff, (1408, 1024, 512, 256, 128))

    cs = jnp.zeros((MOD_ROWS, d), F32).at[:bsz].set(c).at[bsz].set(c_ctx)
    mods = _modulation(cs, w_mod, b_mod).reshape(depth * MOD_ROWS * N_MOD, 1, d)

    w_in_b, w_out_b = w_in.astype(BF16), w_out.astype(BF16)
    w_up_b, w_down_b = ffn_w_up.astype(BF16), ffn_w_down.astype(BF16)
    cos, sin = _rope_tables(seq)
    dec = jnp.broadcast_to(jnp.stack([ret_decay_f, ret_decay_b], axis=1).astype(F32)[:, :, :, None, None],
                           (depth, 2, heads, 1, HEAD_DIM))
    n1 = norm1_g.reshape(depth, 1, d)
    n2 = norm2_g.reshape(depth, 1, d)
    cb = conv_dw_b.reshape(depth, 1, cw)
    clg = conv_ln_g.reshape(depth, 1, cw)
    clb = conv_ln_b.reshape(depth, 1, cw)
    w9 = ffn_dw_w.reshape(depth, 9, dff)
    fb = ffn_dw_b.reshape(depth, 1, dff)
    zero_state = jnp.zeros((bsz, heads, HEAD_DIM, HEAD_DIM), F32)

    tiles_all, tiles_lat = t // tm, n_lat // tm
    res_all, res_lat = t // tm_res, n_lat // tm_res
    mix_w = rw + cw
    col_a = 4 * rw // cw

    x_all, h = _first_norm(x.reshape(n_lat, d), ctx.reshape(n_ctx, d), n1, mods, 0, tm_res, seq, bsz)
    out = None
    for l in range(depth):
        last = l == depth - 1
        p = _projection(h, w_in_b, l, tiles_all, tm, tn_in)
        mix, s_f, s_b = _retention(p, dec, l, zero_state, zero_state, None, None, base_row=n_lat, nseq=bsz, seq=lc,
                                   heads=heads, mix_width=mix_w)
        mix, _, _ = _retention(p, dec, l, s_f, s_b, mix, (cos, sin), base_row=0, nseq=bsz, seq=seq, heads=heads,
                               mix_width=mix_w)
        mix = _conformer_conv(p, conv_dw_w, cb, clg, clb, l, mix, base_row=0, nseq=bsz, seq=seq, col_a=col_a,
                              col_out=rw // cw)
        if not last:
            mix = _conformer_conv(p, conv_dw_w, cb, clg, clb, l, mix, base_row=n_lat, nseq=bsz, seq=lc, col_a=col_a,
                                  col_out=rw // cw)
        x_all, h2 = _mm_residual(mix, w_out_b, l, x_all, mods, 2, n2, l, (l, 3, 4),
                                 n_tiles=res_lat if last else res_all, tm=tm_res, tk=mix_w, seq=seq, batch=bsz)
        gv = _projection(h2, w_up_b, l, tiles_lat if last else tiles_all, tm, tn_up)
        if not last:
            act_ctx = _ffn_act(gv, w9, fb, l, None, base_row=n_lat, nseq=bsz, seq=lc, width=lc, vertical=False)
            x_all, h = _ffn_down_fused(gv, w9, fb, w_down_b, l, x_all, mods, 5, n1, l + 1, (l + 1, 0, 1),
                                       n_lat=n_lat, seq=seq, batch=bsz)
            x_all, h = _mm_residual(act_ctx, w_down_b, l, x_all, mods, 5, n1, l + 1, (l + 1, 0, 1),
                                    n_tiles=res_all - res_lat, tm=tm_res, tk=tk_down, seq=seq, batch=bsz,
                                    tile0=res_lat, h_prev=h)
        else:
            out = _ffn_down_fused(gv, w9, fb, w_down_b, l, x_all, mods, 5, final_norm_g, 0, None,
                                  n_lat=n_lat, seq=seq, batch=bsz)
    return out.reshape(bsz, seq, d)


def _gcd(a, b):
    while b:
        a, b = b, a % b
    return a
```

```python
import functools

import jax
import jax.numpy as jnp
from jax import lax
from jax.experimental import pallas as pl
from jax.experimental.pallas import tpu as pltpu

F32 = jnp.float32
BF16 = jnp.bfloat16

GRID_W = 64
HEAD_DIM = 128
CHUNK = 128
N_MOD = 6
ROPE_THETA = 10000.0
EPS = 1e-6
MOD_ROWS = 8
CONV_HALO = 16
ACT_ROWS = 64
MXU_COLS = 256
MXU_K = 512
ROW_BLOCK = 16
LOOP_UNROLL = 16
VMEM_LIMIT = 52 * 1024 * 1024

ACT_DTYPE = BF16


def _pick(n, candidates):
    for c in candidates:
        if n % c == 0:
            return c
    raise ValueError(f"no tile for {n} in {candidates}")


def _params(sem):
    return pltpu.CompilerParams(dimension_semantics=sem, vmem_limit_bytes=VMEM_LIMIT)


def _sigmoid(x):
    return jax.nn.sigmoid(x)


def _norm_mod(x, g, sh, sc):
    xn = x * lax.rsqrt(jnp.mean(x * x, axis=-1, keepdims=True) + EPS)
    return (xn * g) * (1.0 + sc) + sh


def _mod_kernel(c_ref, w_ref, b_ref, o_ref):
    c = c_ref[...]
    s = c * _sigmoid(c)
    o_ref[...] = jnp.dot(s.astype(BF16), w_ref[...].astype(BF16), preferred_element_type=F32) + b_ref[...]


def _modulation(cs, w_mod, b_mod):
    depth, d, n = w_mod.shape
    tn = _pick(n, (1024, 512, 256, 128))
    return pl.pallas_call(
        _mod_kernel,
        grid=(depth, n // tn),
        in_specs=[
            pl.BlockSpec((MOD_ROWS, d), lambda l, j: (0, 0)),
            pl.BlockSpec((None, d, tn), lambda l, j: (l, 0, j)),
            pl.BlockSpec((None, 1, tn), lambda l, j: (l, 0, j)),
        ],
        out_specs=pl.BlockSpec((None, MOD_ROWS, tn), lambda l, j: (l, 0, j)),
        out_shape=jax.ShapeDtypeStruct((depth, MOD_ROWS, n), F32),
        compiler_params=_params(("arbitrary", "arbitrary")),
        name="modulation",
    )(cs, w_mod, b_mod.reshape(depth, 1, n))


def _mod_spec(layer, which, tm, seq, batch, d, tile0=0, tile_of=None):
    def imap(i, *rest):
        tile = (i + tile0) if tile_of is None else tile_of(i, *rest)
        grp = jnp.minimum((tile * tm) // seq, batch)
        return ((layer * MOD_ROWS + grp) * N_MOD + which, 0, 0)
    return pl.BlockSpec((None, 1, d), imap)


def _first_norm_kernel(xl_ref, xc_ref, g_ref, sh_ref, sc_ref, x_ref, h_ref, *, n_lat_tiles):
    g, sh, sc = g_ref[...], sh_ref[...], sc_ref[...]

    def run(src_ref):
        def body(r, carry):
            rows = pl.ds(pl.multiple_of(r * ROW_BLOCK, ROW_BLOCK), ROW_BLOCK)
            xr = src_ref[rows, :]
            x_ref[rows, :] = xr
            h_ref[rows, :] = _norm_mod(xr, g, sh, sc).astype(h_ref.dtype)
            return carry

        lax.fori_loop(0, x_ref.shape[0] // ROW_BLOCK, body, 0, unroll=LOOP_UNROLL)

    @pl.when(pl.program_id(0) < n_lat_tiles)
    def _():
        run(xl_ref)

    @pl.when(pl.program_id(0) >= n_lat_tiles)
    def _():
        run(xc_ref)


def _first_norm(x_lat, x_ctx, norm_g, mods, layer, tm, seq, batch):
    n_lat, d = x_lat.shape
    n_ctx = x_ctx.shape[0]
    t = n_lat + n_ctx
    n_lat_tiles, n_ctx_tiles = n_lat // tm, n_ctx // tm
    return pl.pallas_call(
        functools.partial(_first_norm_kernel, n_lat_tiles=n_lat_tiles),
        grid=(t // tm,),
        in_specs=[
            pl.BlockSpec((tm, d), lambda i: (jnp.minimum(i, n_lat_tiles - 1), 0)),
            pl.BlockSpec((tm, d), lambda i: (jnp.clip(i - n_lat_tiles, 0, n_ctx_tiles - 1), 0)),
            pl.BlockSpec((None, 1, d), lambda i: (layer, 0, 0)),
            _mod_spec(layer, 0, tm, seq, batch, d),
            _mod_spec(layer, 1, tm, seq, batch, d),
        ],
        out_specs=[pl.BlockSpec((tm, d), lambda i: (i, 0)), pl.BlockSpec((tm, d), lambda i: (i, 0))],
        out_shape=[jax.ShapeDtypeStruct((t, d), F32), jax.ShapeDtypeStruct((t, d), BF16)],
        compiler_params=_params(("arbitrary",)),
        name="first_norm",
    )(x_lat, x_ctx, norm_g, mods, mods)


def _proj_kernel(a_ref, w_ref, o_ref):
    o_ref[...] = jnp.dot(a_ref[...], w_ref[...], preferred_element_type=F32).astype(o_ref.dtype)


def _projection(a, w, layer, n_tiles, tm, tn):
    t, kdim = a.shape
    n = w.shape[2]
    return pl.pallas_call(
        _proj_kernel,
        grid=(n_tiles, n // tn),
        in_specs=[
            pl.BlockSpec((tm, kdim), lambda i, j: (i, 0)),
            pl.BlockSpec((None, kdim, tn), lambda i, j: (layer, 0, j)),
        ],
        out_specs=pl.BlockSpec((tm, tn), lambda i, j: (i, j)),
        out_shape=jax.ShapeDtypeStruct((t, n), ACT_DTYPE),
        compiler_params=_params(("arbitrary", "arbitrary")),
        name="projection",
    )(a, w)


def _log_sigmoid(x):
    return jnp.minimum(x, 0.0) - jnp.log1p(jnp.exp(-jnp.abs(x)))


def _rope(x, cos, sin_signed):
    lane = lax.broadcasted_iota(jnp.int32, x.shape, 1)
    first = (lane % (HEAD_DIM // 2)) < (HEAD_DIM // 4)
    partner = jnp.where(first, pltpu.roll(x, HEAD_DIM - HEAD_DIM // 4, 1), pltpu.roll(x, HEAD_DIM // 4, 1))
    return x * cos + partner * sin_signed


def _retention_kernel(dec_ref, q_ref, k_ref, v_ref, g_ref, s0f_ref, s0b_ref, *rest, nseg, nchunk, aliased, rope):
    if rope:
        cos_ref, sin_ref = rest[:2]
        rest = rest[2:]
    if aliased:
        rest = rest[1:]
    o_ref, sf_ref, sb_ref, of_scr, state_scr, decay_scr = rest
    s = pl.program_id(2)
    c_len = CHUNK
    lseg = nchunk * c_len
    scale = HEAD_DIM ** -0.5

    @pl.when(s == 0)
    def _init():
        row = lax.broadcasted_iota(jnp.int32, (c_len, HEAD_DIM), 0).astype(F32)
        col = lax.broadcasted_iota(jnp.int32, (c_len, HEAD_DIM), 1).astype(F32)
        for d in range(2):
            lg = _log_sigmoid(dec_ref[d])
            diff = (row - col) if d == 0 else (col - row)
            decay_scr[d, 0] = jnp.where(diff >= 0, jnp.exp(lg * jnp.maximum(diff, 0.0)), 0.0)
            if d == 0:
                decay_scr[d, 1] = jnp.exp(lg * (row + 1.0))
                decay_scr[d, 2] = jnp.exp(lg * (c_len - 1.0 - row))
            else:
                decay_scr[d, 1] = jnp.exp(lg * (c_len - row))
                decay_scr[d, 2] = jnp.exp(lg * row)
            decay_scr[d, 3] = jnp.broadcast_to(jnp.exp(lg * float(c_len)), (c_len, HEAD_DIM))
        state_scr[...] = s0f_ref[...]

    @pl.when(s == nseg)
    def _init_b():
        state_scr[...] = s0b_ref[...]

    def chunk(c, d, state):
        rows = slice(c * c_len, (c + 1) * c_len)
        qf = q_ref[rows, :].astype(F32)
        kf = k_ref[rows, :].astype(F32)
        if rope:
            cos, sin = cos_ref[rows, :], sin_ref[rows, :]
            qf, kf = _rope(qf, cos, sin), _rope(kf, cos, sin)
        q = qf.astype(BF16)
        kf = kf * scale
        v = v_ref[rows, :].astype(BF16)
        sc = lax.dot_general(q, kf.astype(BF16), (((1,), (1,)), ((), ())), preferred_element_type=F32)
        sc = sc * decay_scr[d, 0]
        o = jnp.dot(sc.astype(BF16), v, preferred_element_type=F32)
        o = o + jnp.dot(q, state.astype(BF16), preferred_element_type=F32) * decay_scr[d, 1]
        kd = (kf * decay_scr[d, 2]).astype(BF16)
        state = state * decay_scr[d, 3] + lax.dot_general(kd, v, (((0,), (0,)), ((), ())), preferred_element_type=F32)
        return o, state

    @pl.when(s < nseg)
    def _fwd():
        state = state_scr[...]
        for c in range(nchunk):
            o, state = chunk(c, 0, state)
            of_scr[pl.ds(pl.multiple_of(s * lseg + c * c_len, c_len), c_len), :] = o
        state_scr[...] = state

        @pl.when(s == nseg - 1)
        def _():
            sf_ref[...] = state

    @pl.when(s >= nseg)
    def _bwd():
        seg = 2 * nseg - 1 - s
        state = state_scr[...]
        for c in reversed(range(nchunk)):
            o, state = chunk(c, 1, state)
            o = o + of_scr[pl.ds(pl.multiple_of(seg * lseg + c * c_len, c_len), c_len), :]
            o = o * lax.rsqrt(jnp.mean(o * o, axis=-1, keepdims=True) + EPS)
            g = g_ref[c * c_len:(c + 1) * c_len, :].astype(F32)
            o_ref[c * c_len:(c + 1) * c_len, :] = (g * _sigmoid(g) * o).astype(o_ref.dtype)
        state_scr[...] = state

        @pl.when(s == 2 * nseg - 1)
        def _():
            sb_ref[...] = state


def _retention(p, dec, layer, s0f, s0b, mix, rope, *, base_row, nseq, seq, heads, mix_width):
    t = p.shape[0]
    lseg = _pick(seq, (8192, 4096, 2048, 1024, 512, 256, 128))
    nseg = seq // lseg
    nchunk = lseg // CHUNK
    base_blk = base_row // lseg
    assert base_row % lseg == 0

    def seg_of(s):
        return jnp.where(s < nseg, s, 2 * nseg - 1 - s)

    def out_seg_of(s):
        return jnp.where(s < nseg, nseg - 1, 2 * nseg - 1 - s)

    def in_spec(col0):
        return pl.BlockSpec((lseg, HEAD_DIM), lambda b, h, s: (base_blk + b * nseg + seg_of(s), col0 + h))

    state_spec = pl.BlockSpec((None, None, HEAD_DIM, HEAD_DIM), lambda b, h, s: (b, h, 0, 0))
    in_specs = [
        pl.BlockSpec((None, 2, None, 1, HEAD_DIM), lambda b, h, s: (layer, 0, h, 0, 0)),
        in_spec(0), in_spec(heads), in_spec(2 * heads),
        pl.BlockSpec((lseg, HEAD_DIM), lambda b, h, s: (base_blk + b * nseg + out_seg_of(s), 3 * heads + h)),
        state_spec, state_spec,
    ]
    args = [dec, p, p, p, p, s0f, s0b]
    if rope is not None:
        in_specs += [pl.BlockSpec((lseg, HEAD_DIM), lambda b, h, s: (seg_of(s), 0))] * 2
        args += list(rope)
    aliases = {}
    if mix is not None:
        in_specs.append(pl.BlockSpec(memory_space=pl.ANY))
        args.append(mix)
        aliases = {len(args) - 1: 0}
    state_shape = jax.ShapeDtypeStruct((nseq, heads, HEAD_DIM, HEAD_DIM), F32)
    return pl.pallas_call(
        functools.partial(_retention_kernel, nseg=nseg, nchunk=nchunk, aliased=mix is not None,
                          rope=rope is not None),
        grid=(nseq, heads, 2 * nseg),
        in_specs=in_specs,
        out_specs=[
            pl.BlockSpec((lseg, HEAD_DIM), lambda b, h, s: (base_blk + b * nseg + out_seg_of(s), h)),
            state_spec, state_spec,
        ],
        out_shape=[jax.ShapeDtypeStruct((t, mix_width), BF16), state_shape, state_shape],
        scratch_shapes=[
            pltpu.VMEM((seq, HEAD_DIM), F32),
            pltpu.VMEM((HEAD_DIM, HEAD_DIM), F32),
            pltpu.VMEM((2, 4, CHUNK, HEAD_DIM), F32),
        ],
        input_output_aliases=aliases,
        compiler_params=_params(("arbitrary", "arbitrary", "arbitrary")),
        name="retention",
    )(*args)


def _cconv_kernel(am_ref, ap_ref, an_ref, bm_ref, bp_ref, bn_ref, w_ref, b_ref, lg_ref, lb_ref, mix_ref,
                  o_ref, u_scr, y_scr, us_scr, *, taps):
    del mix_ref
    j = pl.program_id(1)
    nj = pl.num_programs(1)
    tl, cw = o_ref.shape
    halo = CONV_HALO

    def glu(a_ref, b_ref_):
        return a_ref[...].astype(F32) * _sigmoid(b_ref_[...].astype(F32))

    u_scr[halo:halo + tl, :] = glu(am_ref, bm_ref)
    u_scr[0:halo, :] = jnp.where(j > 0, glu(ap_ref, bp_ref), 0.0)
    u_scr[halo + tl:halo + tl + halo, :] = jnp.where(j < nj - 1, glu(an_ref, bn_ref), 0.0)

    off0 = halo - taps // 2
    rb = 64 if tl % 64 == 0 else tl

    n_ext = us_scr.shape[1]

    def lane_body(c, carry):
        lanes = pl.ds(pl.multiple_of(c * 128, 128), 128)
        for ph in range(1, 8):
            us_scr[ph - 1] = u_scr[ph:ph + n_ext, lanes]
        for r in range(tl // rb):
            acc = jnp.zeros((rb, 128), F32)
            for k in range(taps):
                ph, a0 = (off0 + k) % 8, r * rb + 8 * ((off0 + k) // 8)
                src = u_scr[a0:a0 + rb, lanes] if ph == 0 else us_scr[ph - 1, a0:a0 + rb, :]
                acc = acc + w_ref[k:k + 1, lanes] * src
            y_scr[r * rb:(r + 1) * rb, lanes] = acc
        return carry

    lax.fori_loop(0, cw // 128, lane_body, 0)

    bias, ln_g, ln_b = b_ref[...], lg_ref[...], lb_ref[...]

    def row_body(r, carry):
        rows = pl.ds(pl.multiple_of(r * ROW_BLOCK, ROW_BLOCK), ROW_BLOCK)
        y = y_scr[rows, :] + bias
        mu = jnp.mean(y, axis=-1, keepdims=True)
        yc = y - mu
        var = jnp.mean(yc * yc, axis=-1, keepdims=True)
        z = yc * lax.rsqrt(var + EPS) * ln_g + ln_b
        o_ref[rows, :] = (z * _sigmoid(z)).astype(o_ref.dtype)
        return carry

    lax.fori_loop(0, tl // ROW_BLOCK, row_body, 0, unroll=LOOP_UNROLL)


def _conformer_conv(p, w, b, ln_g, ln_b, layer, mix, *, base_row, nseq, seq, col_a, col_out):
    t = p.shape[0]
    taps, cw = w.shape[1], w.shape[2]
    tl = _pick(seq, (512, 256, 128))
    nt = seq // tl
    assert base_row % tl == 0 and taps // 2 < CONV_HALO
    base_blk = base_row // tl
    hpt = tl // CONV_HALO
    last_halo = t // CONV_HALO - 1

    def main_spec(col):
        return pl.BlockSpec((tl, cw), lambda s, j: (base_blk + s * nt + j, col))

    def prev_spec(col):
        return pl.BlockSpec((CONV_HALO, cw), lambda s, j: (jnp.maximum((base_blk + s * nt + j) * hpt - 1, 0), col))

    def next_spec(col):
        return pl.BlockSpec((CONV_HALO, cw), lambda s, j: (jnp.minimum((base_blk + s * nt + j + 1) * hpt, last_halo), col))

    vec_spec = pl.BlockSpec((None, 1, cw), lambda s, j: (layer, 0, 0))
    return pl.pallas_call(
        functools.partial(_cconv_kernel, taps=taps),
        grid=(nseq, nt),
        in_specs=[
            main_spec(col_a), prev_spec(col_a), next_spec(col_a),
            main_spec(col_a + 1), prev_spec(col_a + 1), next_spec(col_a + 1),
            pl.BlockSpec((None, taps, cw), lambda s, j: (layer, 0, 0)),
            vec_spec, vec_spec, vec_spec,
            pl.BlockSpec(memory_space=pl.ANY),
        ],
        out_specs=pl.BlockSpec((tl, cw), lambda s, j: (base_blk + s * nt + j, col_out)),
        out_shape=jax.ShapeDtypeStruct(mix.shape, mix.dtype),
        scratch_shapes=[pltpu.VMEM((tl + 2 * CONV_HALO, cw), F32), pltpu.VMEM((tl, cw), F32),
                        pltpu.VMEM((7, tl + 8 * ((CONV_HALO + taps // 2) // 8), 128), F32)],
        input_output_aliases={10: 0},
        compiler_params=_params(("arbitrary", "arbitrary")),
        name="conformer_conv",
    )(p, p, p, p, p, p, w, b, ln_g, ln_b, mix)


def _residual_epilogue(x_ref, acc_scr, gate_ref, ng_ref, sh_ref, sc_ref, xo_ref, h_ref, o_ref, clear_acc=False):
    final = o_ref is not None
    gate, ng = gate_ref[...], ng_ref[...]
    if not final:
        sh, sc = sh_ref[...], sc_ref[...]

    def body(r, carry):
        rows = pl.ds(pl.multiple_of(r * ROW_BLOCK, ROW_BLOCK), ROW_BLOCK)
        xn = x_ref[rows, :] + gate * acc_scr[rows, :]
        if clear_acc:
            acc_scr[rows, :] = jnp.zeros((ROW_BLOCK, acc_scr.shape[1]), F32)
        if final:
            o_ref[rows, :] = xn * lax.rsqrt(jnp.mean(xn * xn, axis=-1, keepdims=True) + EPS) * ng
        else:
            xo_ref[rows, :] = xn
            h_ref[rows, :] = _norm_mod(xn, ng, sh, sc).astype(h_ref.dtype)
        return carry

    lax.fori_loop(0, x_ref.shape[0] // ROW_BLOCK, body, 0, unroll=LOOP_UNROLL)


def _mm_res_kernel(a_ref, w_ref, x_ref, gate_ref, ng_ref, *rest, final, h_aliased):
    if final:
        o_ref, acc_scr = rest
    else:
        sh_ref, sc_ref = rest[:2]
        xo_ref, h_ref, acc_scr = rest[3:] if h_aliased else rest[2:]
    k = pl.program_id(1)
    nk = pl.num_programs(1)
    prod = jnp.dot(a_ref[...], w_ref[...], preferred_element_type=F32)

    @pl.when(k == 0)
    def _():
        acc_scr[...] = prod

    @pl.when(k > 0)
    def _():
        acc_scr[...] += prod

    @pl.when(k == nk - 1)
    def _():
        if final:
            _residual_epilogue(x_ref, acc_scr, gate_ref, ng_ref, None, None, None, None, o_ref)
        else:
            _residual_epilogue(x_ref, acc_scr, gate_ref, ng_ref, sh_ref, sc_ref, xo_ref, h_ref, None)


def _mm_residual(a, w, layer, x_all, mods, gate_which, norm_g, norm_layer, next_mods, *, n_tiles, tm, tk, seq, batch,
                 final_rows=None, tile0=0, h_prev=None):
    t, d = x_all.shape
    kdim = a.shape[1]
    final = next_mods is None
    in_specs = [
        pl.BlockSpec((tm, tk), lambda i, k: (i + tile0, k)),
        pl.BlockSpec((None, tk, d), lambda i, k: (layer, k, 0)),
        pl.BlockSpec((tm, d), lambda i, k: (i + tile0, 0)),
        _mod_spec(layer, gate_which, tm, seq, batch, d, tile0),
    ]
    args = [a, w, x_all, mods]
    if final:
        in_specs.append(pl.BlockSpec((1, d), lambda i, k: (0, 0)))
        args.append(norm_g.reshape(1, d))
        out_specs = pl.BlockSpec((tm, d), lambda i, k: (i + tile0, 0))
        out_shape = jax.ShapeDtypeStruct((final_rows, d), F32)
        aliases = {}
    else:
        nl, shw, scw = next_mods
        in_specs += [
            pl.BlockSpec((None, 1, d), lambda i, k: (norm_layer, 0, 0)),
            _mod_spec(nl, shw, tm, seq, batch, d, tile0),
            _mod_spec(nl, scw, tm, seq, batch, d, tile0),
        ]
        args += [norm_g, mods, mods]
        row_spec = pl.BlockSpec((tm, d), lambda i, k: (i + tile0, 0))
        out_specs = [row_spec, row_spec]
        out_shape = [jax.ShapeDtypeStruct((t, d), F32), jax.ShapeDtypeStruct((t, d), BF16)]
        aliases = {2: 0}
        if h_prev is not None:
            in_specs.append(pl.BlockSpec(memory_space=pl.ANY))
            args.append(h_prev)
            aliases[len(args) - 1] = 1
    return pl.pallas_call(
        functools.partial(_mm_res_kernel, final=final, h_aliased=h_prev is not None),
        grid=(n_tiles, kdim // tk),
        in_specs=in_specs,
        out_specs=out_specs,
        out_shape=out_shape,
        scratch_shapes=[pltpu.VMEM((tm, d), F32)],
        input_output_aliases=aliases,
        compiler_params=_params(("arbitrary", "arbitrary")),
        name="matmul_residual",
    )(*args)


def _ffn_act_kernel(*refs, width, vertical, aliased):
    if vertical:
        gm_ref, gp_ref, gn_ref, val_ref, w_ref, b_ref = refs[:6]
        rest = refs[6:]
    else:
        gm_ref, val_ref, w_ref, b_ref = refs[:4]
        rest = refs[4:]
    if aliased:
        rest = rest[1:]
    o_ref, s_scr = rest
    j = pl.program_id(1)
    if not vertical:
        gp_ref = gn_ref = None

    _act_fill(gm_ref, gp_ref, gn_ref, s_scr, j > 0, j < pl.num_programs(1) - 1, width=width, vertical=vertical)
    for r in range(o_ref.shape[0] // ACT_ROWS):
        for c in range(o_ref.shape[1] // 128):
            o_ref[r * ACT_ROWS:(r + 1) * ACT_ROWS, c * 128:(c + 1) * 128] = _act_rows(
                r, c, val_ref, w_ref, b_ref, s_scr, width=width, vertical=vertical).astype(o_ref.dtype)


def _act_fill(gm_ref, gp_ref, gn_ref, s_scr, has_above, has_below, *, width, vertical):
    tl, tf = gm_ref.shape
    pad = 8
    base = pad + (width if vertical else 0)
    s_scr[0:pad, :] = jnp.zeros((pad, tf), F32)
    s_scr[s_scr.shape[0] - pad:, :] = jnp.zeros((pad, tf), F32)
    s_scr[base:base + tl, :] = gm_ref[...].astype(F32)
    if vertical:
        s_scr[pad:pad + width, :] = jnp.where(has_above, gp_ref[...].astype(F32), 0.0)
        s_scr[base + tl:base + tl + width, :] = jnp.where(has_below, gn_ref[...].astype(F32), 0.0)


def _act_rows(r, c, val_ref, w_ref, b_ref, s_scr, *, width, vertical):
    rb, tf = ACT_ROWS, 128
    lanes = slice(c * 128, (c + 1) * 128)
    base = 8 + (width if vertical else 0)
    dys = (-1, 0, 1) if vertical else (0,)
    col = (lax.broadcasted_iota(jnp.int32, (rb, tf), 0) + r * rb) % width
    acc = jnp.zeros((rb, tf), F32)
    for dx in (-1, 0, 1):
        part = jnp.zeros((rb, tf), F32)
        for dy in dys:
            tap = (dy + 1) * 3 + (dx + 1)
            start = base + r * rb + dy * width + dx
            part = part + w_ref[tap:tap + 1, lanes] * s_scr[start:start + rb, lanes]
        if dx == -1:
            part = jnp.where(col != 0, part, 0.0)
        elif dx == 1:
            part = jnp.where(col != width - 1, part, 0.0)
        acc = acc + part
    conv = acc + b_ref[:, lanes]
    val = val_ref[r * rb:(r + 1) * rb, lanes].astype(F32)
    return conv * _sigmoid(conv) * val


def _ffn_act(gv, w9, b, layer, act, *, base_row, nseq, seq, width, vertical):
    t = gv.shape[0]
    dff = w9.shape[2]
    tf = _pick(dff, (512, 256, 128))
    tl = _pick(seq, (512, 256, 128)) if vertical else seq
    assert tl % width == 0 and base_row % tl == 0 and tl % ACT_ROWS == 0
    nt = seq // tl
    nf = dff // tf
    base_blk = base_row // tl
    wpt = tl // width
    last_w = t // width - 1
    main = lambda s, j, f: (base_blk + s * nt + j, f)
    in_specs = [pl.BlockSpec((tl, tf), main)]
    args = [gv]
    if vertical:
        in_specs += [
            pl.BlockSpec((width, tf), lambda s, j, f: (jnp.maximum((base_blk + s * nt + j) * wpt - 1, 0), f)),
            pl.BlockSpec((width, tf), lambda s, j, f: (jnp.minimum((base_blk + s * nt + j + 1) * wpt, last_w), f)),
        ]
        args += [gv, gv]
    in_specs += [
        pl.BlockSpec((tl, tf), lambda s, j, f: (base_blk + s * nt + j, nf + f)),
        pl.BlockSpec((None, 9, tf), lambda s, j, f: (layer, 0, f)),
        pl.BlockSpec((None, 1, tf), lambda s, j, f: (layer, 0, f)),
    ]
    args += [gv, w9, b]
    aliases = {}
    if act is not None:
        in_specs.append(pl.BlockSpec(memory_space=pl.ANY))
        args.append(act)
        aliases = {len(args) - 1: 0}
    rows_scr = tl + 16 + (2 * width if vertical else 0)
    return pl.pallas_call(
        functools.partial(_ffn_act_kernel, width=width, vertical=vertical, aliased=act is not None),
        grid=(nseq, nt, nf),
        in_specs=in_specs,
        out_specs=pl.BlockSpec((tl, tf), main),
        out_shape=jax.ShapeDtypeStruct((t, dff), BF16),
        scratch_shapes=[pltpu.VMEM((rows_scr, tf), F32)],
        input_output_aliases=aliases,
        compiler_params=_params(("arbitrary", "arbitrary", "arbitrary")),
        name="ffn_act",
    )(*args)


def _ffn_down_kernel(gm_ref, gp_ref, gn_ref, val_ref, w9_ref, b9_ref, wd_ref, x_ref, gate_ref, ng_ref, *rest,
                     nk, n_steps, tiles_per_image, final):
    if final:
        o_ref, s_scr, act_a, act_b, acc_scr = rest
    else:
        sh_ref, sc_ref, xo_ref, h_ref, s_scr, act_a, act_b, acc_scr = rest
    s = pl.program_id(0)
    j = (jnp.minimum(s, n_steps - 1) // nk) % tiles_per_image
    consumed_chunk = jnp.maximum(s - 1, 0) % nk
    slots = (act_a, act_b)
    n_rows = gm_ref.shape[0] // ACT_ROWS
    n_lane = gm_ref.shape[1] // 128
    n_cols = acc_scr.shape[1] // MXU_COLS
    tk = gm_ref.shape[1]
    k_parts = [slice(k0, min(k0 + MXU_K, tk)) for k0 in range(0, tk, MXU_K)]

    def step(produce_slot, consume_slot):
        def consume(n):
            ks, cols = k_parts[n // n_cols], slice((n % n_cols) * MXU_COLS, (n % n_cols + 1) * MXU_COLS)
            acc_scr[:, cols] += jnp.dot(slots[consume_slot][:, ks], wd_ref[ks, cols], preferred_element_type=F32)

        def produce(r, c):
            slots[produce_slot][r * ACT_ROWS:(r + 1) * ACT_ROWS, c * 128:(c + 1) * 128] = _act_rows(
                r, c, val_ref, w9_ref, b9_ref, s_scr, width=GRID_W, vertical=True).astype(BF16)

        pieces = [(r, c) for r in range(n_rows) for c in range(n_lane)] if produce_slot is not None else []
        n_consume = n_cols * len(k_parts) if consume_slot is not None else 0
        done = 0
        if n_consume:
            consume(0)
            done = 1
        if produce_slot is not None:
            _act_fill(gm_ref, gp_ref, gn_ref, s_scr, j > 0, j < tiles_per_image - 1, width=GRID_W, vertical=True)
        for idx, (r, c) in enumerate(pieces):
            produce(r, c)
            target = 1 + (idx + 1) * (n_consume - 1) // len(pieces) if n_consume else 0
            while done < target:
                consume(done)
                done += 1
        while done < n_consume:
            consume(done)
            done += 1

    @pl.when(s == 0)
    def _():
        acc_scr[...] = jnp.zeros(acc_scr.shape, F32)
        step(0, None)

    middle = jnp.logical_and(s > 0, s < n_steps)

    @pl.when(jnp.logical_and(middle, s % 2 == 0))
    def _():
        step(0, 1)

    @pl.when(jnp.logical_and(middle, s % 2 == 1))
    def _():
        step(1, 0)

    @pl.when(s == n_steps)
    def _():
        step(None, (n_steps - 1) % 2)

    @pl.when(jnp.logical_and(s > 0, consumed_chunk == nk - 1))
    def _():
        if final:
            _residual_epilogue(x_ref, acc_scr, gate_ref, ng_ref, None, None, None, None, o_ref, clear_acc=True)
        else:
            _residual_epilogue(x_ref, acc_scr, gate_ref, ng_ref, sh_ref, sc_ref, xo_ref, h_ref, None, clear_acc=True)


def _ffn_down_fused(gv, w9, b9, w_down, layer, x_all, mods, gate_which, norm_g, norm_layer, next_mods, *, n_lat, seq,
                    batch):
    t, d = x_all.shape
    dff = w9.shape[2]
    tk = _pick(dff, (1408, 512, 256, 128))
    tl = _pick(seq, (512, 256, 128))
    nk = dff // tk
    n_steps = (n_lat // tl) * nk
    assert tl % GRID_W == 0 and tk % 128 == 0
    wpt = tl // GRID_W
    last_w = t // GRID_W - 1
    final = next_mods is None

    def produced(s):
        sp = jnp.minimum(s, n_steps - 1)
        return sp // nk, sp % nk

    def consumed(s):
        sc = jnp.maximum(s - 1, 0)
        return sc // nk, sc % nk

    consumed_tile = lambda s: consumed(s)[0]
    in_specs = [
        pl.BlockSpec((tl, tk), lambda s: produced(s)),
        pl.BlockSpec((GRID_W, tk), lambda s: (jnp.maximum(produced(s)[0] * wpt - 1, 0), produced(s)[1])),
        pl.BlockSpec((GRID_W, tk), lambda s: (jnp.minimum((produced(s)[0] + 1) * wpt, last_w), produced(s)[1])),
        pl.BlockSpec((tl, tk), lambda s: (produced(s)[0], nk + produced(s)[1])),
        pl.BlockSpec((None, 9, tk), lambda s: (layer, 0, produced(s)[1])),
        pl.BlockSpec((None, 1, tk), lambda s: (layer, 0, produced(s)[1])),
        pl.BlockSpec((None, tk, d), lambda s: (layer, consumed(s)[1], 0)),
        pl.BlockSpec((tl, d), lambda s: (consumed_tile(s), 0)),
        _mod_spec(layer, gate_which, tl, seq, batch, d, tile_of=consumed_tile),
    ]
    args = [gv, gv, gv, gv, w9, b9, w_down, x_all, mods]
    row_spec = pl.BlockSpec((tl, d), lambda s: (consumed_tile(s), 0))
    if final:
        in_specs.append(pl.BlockSpec((1, d), lambda s: (0, 0)))
        args.append(norm_g.reshape(1, d))
        out_specs, out_shape, aliases = row_spec, jax.ShapeDtypeStruct((n_lat, d), F32), {}
    else:
        nl, shw, scw = next_mods
        in_specs += [
            pl.BlockSpec((None, 1, d), lambda s: (norm_layer, 0, 0)),
            _mod_spec(nl, shw, tl, seq, batch, d, tile_of=consumed_tile),
            _mod_spec(nl, scw, tl, seq, batch, d, tile_of=consumed_tile),
        ]
        args += [norm_g, mods, mods]
        out_specs = [row_spec, row_spec]
        out_shape = [jax.ShapeDtypeStruct((t, d), F32), jax.ShapeDtypeStruct((t, d), BF16)]
        aliases = {7: 0}
    return pl.pallas_call(
        functools.partial(_ffn_down_kernel, nk=nk, n_steps=n_steps, tiles_per_image=seq // tl, final=final),
        grid=(n_steps + 1,),
        in_specs=in_specs,
        out_specs=out_specs,
        out_shape=out_shape,
        scratch_shapes=[
            pltpu.VMEM((tl + 2 * GRID_W + 16, tk), F32),
            pltpu.VMEM((tl, tk), BF16),
            pltpu.VMEM((tl, tk), BF16),
            pltpu.VMEM((tl, d), F32),
        ],
        input_output_aliases=aliases,
        compiler_params=_params(("arbitrary",)),
        name="ffn_down_fused",
    )(*args)


def _rope_tables(seq):
    t = jnp.arange(seq)
    quarter = HEAD_DIM // 4
    inv_freq = 1.0 / (ROPE_THETA ** (jnp.arange(0, quarter, dtype=F32) / quarter))
    ang_r = (t // GRID_W).astype(F32)[:, None] * inv_freq[None, :]
    ang_c = (t % GRID_W).astype(F32)[:, None] * inv_freq[None, :]
    cos = jnp.concatenate([jnp.cos(ang_r)] * 2 + [jnp.cos(ang_c)] * 2, axis=-1)
    sin = jnp.concatenate([-jnp.sin(ang_r), jnp.sin(ang_r), -jnp.sin(ang_c), jnp.sin(ang_c)], axis=-1)
    return cos, sin


def kernel(x, c, ctx, c_ctx, w_mod, b_mod, norm1_g, norm2_g, w_in, ret_decay_f, ret_decay_b, conv_dw_w, conv_dw_b,
           conv_ln_g, conv_ln_b, w_out, ffn_w_up, ffn_dw_w, ffn_dw_b, ffn_w_down, final_norm_g):
    bsz, seq, d = x.shape
    lc = ctx.shape[1]
    depth = w_mod.shape[0]
    heads = ret_decay_f.shape[1]
    rw = heads * HEAD_DIM
    cw = conv_dw_w.shape[2]
    dff = ffn_dw_b.shape[1]
    n_lat, n_ctx = bsz * seq, bsz * lc
    t = n_lat + n_ctx
    assert bsz + 1 <= MOD_ROWS and rw % cw == 0 and seq % GRID_W == 0 and w_mod.shape[2] == N_MOD * d

    tm = _pick(_gcd(seq, n_ctx), (1024, 512, 256, 128))
    tm_res = min(tm, 512)
    tn_in = _pick(w_in.shape[2], (1024, 512, 256, 128))
    tn_up = _pick(2 * dff, (1024, 512, 256, 128))
    tk_down = _pick(dff, (1408, 1024, 512, 256, 128))

    cs = jnp.zeros((MOD_ROWS, d), F32).at[:bsz].set(c).at[bsz].set(c_ctx)
    mods = _modulation(cs, w_mod, b_mod).reshape(depth * MOD_ROWS * N_MOD, 1, d)

    w_in_b, w_out_b = w_in.astype(BF16), w_out.astype(BF16)
    w_up_b, w_down_b = ffn_w_up.astype(BF16), ffn_w_down.astype(BF16)
    cos, sin = _rope_tables(seq)
    dec = jnp.broadcast_to(jnp.stack([ret_decay_f, ret_decay_b], axis=1).astype(F32)[:, :, :, None, None],
                           (depth, 2, heads, 1, HEAD_DIM))
    n1 = norm1_g.reshape(depth, 1, d)
    n2 = norm2_g.reshape(depth, 1, d)
    cb = conv_dw_b.reshape(depth, 1, cw)
    clg = conv_ln_g.reshape(depth, 1, cw)
    clb = conv_ln_b.reshape(depth, 1, cw)
    w9 = ffn_dw_w.reshape(depth, 9, dff)
    fb = ffn_dw_b.reshape(depth, 1, dff)
    zero_state = jnp.zeros((bsz, heads, HEAD_DIM, HEAD_DIM), F32)

    tiles_all, tiles_lat = t // tm, n_lat // tm
    res_all, res_lat = t // tm_res, n_lat // tm_res
    mix_w = rw + cw
    col_a = 4 * rw // cw

    x_all, h = _first_norm(x.reshape(n_lat, d), ctx.reshape(n_ctx, d), n1, mods, 0, tm_res, seq, bsz)
    out = None
    for l in range(depth):
        last = l == depth - 1
        p = _projection(h, w_in_b, l, tiles_all, tm, tn_in)
        mix, s_f, s_b = _retention(p, dec, l, zero_state, zero_state, None, None, base_row=n_lat, nseq=bsz, seq=lc,
                                   heads=heads, mix_width=mix_w)
        mix, _, _ = _retention(p, dec, l, s_f, s_b, mix, (cos, sin), base_row=0, nseq=bsz, seq=seq, heads=heads,
                               mix_width=mix_w)
        mix = _conformer_conv(p, conv_dw_w, cb, clg, clb, l, mix, base_row=0, nseq=bsz, seq=seq, col_a=col_a,
                              col_out=rw // cw)
        if not last:
            mix = _conformer_conv(p, conv_dw_w, cb, clg, clb, l, mix, base_row=n_lat, nseq=bsz, seq=lc, col_a=col_a,
                                  col_out=rw // cw)
        x_all, h2 = _mm_residual(mix, w_out_b, l, x_all, mods, 2, n2, l, (l, 3, 4),
                                 n_tiles=res_lat if last else res_all, tm=tm_res, tk=mix_w, seq=seq, batch=bsz)
        gv = _projection(h2, w_up_b, l, tiles_lat if last else tiles_all, tm, tn_up)
        if not last:
            act_ctx = _ffn_act(gv, w9, fb, l, None, base_row=n_lat, nseq=bsz, seq=lc, width=lc, vertical=False)
            x_all, h = _ffn_down_fused(gv, w9, fb, w_down_b, l, x_all, mods, 5, n1, l + 1, (l + 1, 0, 1),
                                       n_lat=n_lat, seq=seq, batch=bsz)
            x_all, h = _mm_residual(act_ctx, w_down_b, l, x_all, mods, 5, n1, l + 1, (l + 1, 0, 1),
                                    n_tiles=res_all - res_lat, tm=tm_res, tk=tk_down, seq=seq, batch=bsz,
                                    tile0=res_lat, h_prev=h)
        else:
            out = _ffn_down_fused(gv, w9, fb, w_down_b, l, x_all, mods, 5, final_norm_g, 0, None,
                                  n_lat=n_lat, seq=seq, batch=bsz)
    return out.reshape(bsz, seq, d)


def _gcd(a, b):
    while b:
        a, b = b, a % b
    return a
```
